```python
import jax, jax.numpy as jnp
from jax import lax
import numpy as np

D_MODEL = 1024
BATCH = 8
SEQ = 2048
DEPTH = 2
DEC_BATCH = 128
DEC_SEQ = 4
PAST_LEN = 16384
PAGE_SIZE = 128

CHUNK = 128
N_GROUPS_A = 8
D_A = 1024
GROUP_A = D_A // N_GROUPS_A
D_B = 1024
CONV_W = 3
D_PLE = 256
N_KEYS = 128
N_EXPERTS = N_KEYS * N_KEYS
PEER_HEADS = 8
D_QUERY = 256
D_HALF = D_QUERY // 2
TOPK = 16
PEER_BLOCK = 256
EPS = 1e-6
D_IN = 2 * D_A + 3 * D_B + 2 * D_MODEL

kernel_name = 'hybrid_gmlp_shortconv_peer_decoder_step'


def _rmsnorm(x, g):
    xf = x.astype(jnp.float32)
    y = xf * lax.rsqrt(jnp.mean(xf * xf, axis=-1, keepdims=True) + EPS)
    return (y * g.astype(jnp.float32)).astype(x.dtype)


def _chunk_gmlp(u, v, w_s, b_s):
    bsz, L, _ = v.shape
    n_chunks = -(-L // CHUNK)
    pad = n_chunks * CHUNK - L
    vp = jnp.pad(v, ((0, 0), (0, pad), (0, 0)))
    vc = vp.reshape(bsz, n_chunks, CHUNK, N_GROUPS_A, GROUP_A)
    mask = jnp.tril(jnp.ones((CHUNK, CHUNK), dtype=bool))
    w = jnp.where(mask[None], w_s, jnp.zeros_like(w_s))
    s = jnp.einsum('gts,bcsgd->bctgd', w, vc) + b_s.T[None, None, :, :, None]
    s = s.reshape(bsz, n_chunks * CHUNK, D_A)[:, :L]
    return u * s


def _short_conv(xc, prev, conv_w):
    L = xc.shape[1]
    xp = jnp.concatenate([prev.astype(xc.dtype), xc], axis=1)
    y = conv_w[0] * xp[:, :L] + conv_w[1] * xp[:, 1:L + 1] + conv_w[2] * xp[:, 2:]
    return y, xp[:, L:]


def _peer(h, w_q, k1, k2, u_tab, v_tab):
    bsz, L, d = h.shape
    T = bsz * L
    nb = -(-T // PEER_BLOCK)
    hf = jnp.pad(h.reshape(T, d), ((0, nb * PEER_BLOCK - T), (0, 0)))
    hb = hf.reshape(nb, PEER_BLOCK, d)

    def block_fn(xb):
        q = (xb @ w_q).reshape(PEER_BLOCK, PEER_HEADS, D_QUERY)
        s1 = jnp.einsum('thd,kd->thk', q[..., :D_HALF], k1).astype(jnp.float32)
        s2 = jnp.einsum('thd,kd->thk', q[..., D_HALF:], k2).astype(jnp.float32)
        t1, i1 = lax.top_k(s1, TOPK)
        t2, i2 = lax.top_k(s2, TOPK)
        cand = (t1[..., :, None] + t2[..., None, :]).reshape(PEER_BLOCK, PEER_HEADS, TOPK * TOPK)
        ts, ic = lax.top_k(cand, TOPK)
        e1 = jnp.take_along_axis(i1, ic // TOPK, axis=-1)
        e2 = jnp.take_along_axis(i2, ic % TOPK, axis=-1)
        eid = e1 * N_KEYS + e2
        g = jax.nn.softmax(ts, axis=-1)
        ue = u_tab[eid]
        a = jax.nn.gelu(jnp.einsum('thkd,td->thk', ue, xb).astype(jnp.float32))
        ve = v_tab[eid]
        return jnp.einsum('thk,thkd->td', (g * a).astype(xb.dtype), ve)

    out = lax.map(block_fn, hb).reshape(nb * PEER_BLOCK, d)[:T]
    return out.reshape(bsz, L, d)


def _layer(x, pe, conv_prev, ln1, w_in, ln_v, w_s, b_s, conv_w, w_a, w_b, w_o,
           ln2, w_q, k1, k2, u_tab, v_tab, ln3, w_pg, w_ple):
    h = _rmsnorm(x, ln1)
    z = h @ w_in
    idx = [D_A, 2 * D_A, 2 * D_A + D_B, 2 * D_A + 2 * D_B, 2 * D_A + 3 * D_B,
           2 * D_A + 3 * D_B + D_MODEL]
    u, v, bg, cg, xin, ga, gb = jnp.split(z, idx, axis=-1)
    u = jax.nn.gelu(u)
    v = _rmsnorm(jax.nn.gelu(v), ln_v)
    ya = _chunk_gmlp(u, v, w_s, b_s) @ w_a
    yc, conv_new = _short_conv(cg * xin, conv_prev, conv_w)
    yb = (bg * yc) @ w_b
    x = x + (jax.nn.sigmoid(ga) * ya + jax.nn.sigmoid(gb) * yb) @ w_o
    x = x + _peer(_rmsnorm(x, ln2), w_q, k1, k2, u_tab, v_tab)
    x = x + jax.nn.sigmoid(_rmsnorm(x, ln3) @ w_pg) * (pe @ w_ple)
    return x, v, conv_new


def setup_inputs(seed: int = 0) -> dict:
    key = jax.random.key(seed)
    ks = jax.random.split(key, 32)
    f32 = jnp.float32
    nrm = lambda k, s, sc: (jax.random.normal(k, s, f32) * sc).astype(f32)
    gain = lambda k, s: 1.0 + 0.02 * jax.random.normal(k, s, f32)
    return {
        'x_prompt': nrm(ks[0], (BATCH, SEQ, D_MODEL), 1.0),
        'x_sample': nrm(ks[1], (DEC_BATCH, DEC_SEQ, D_MODEL), 1.0),
        'state_conv': nrm(ks[2], (DEPTH, DEC_BATCH, CONV_W - 1, D_B), 1.0),
        'p_prompt': nrm(ks[3], (DEPTH, BATCH, SEQ, D_PLE), 1.0),
        'p_sample': nrm(ks[4], (DEPTH, DEC_BATCH, DEC_SEQ, D_PLE), 1.0),
        'ln1': gain(ks[5], (DEPTH, D_MODEL)),
        'w_in': nrm(ks[6], (DEPTH, D_MODEL, D_IN), D_MODEL ** -0.5),
        'ln_v': gain(ks[7], (DEPTH, D_A)),
        'w_s': nrm(ks[8], (DEPTH, N_GROUPS_A, CHUNK, CHUNK), CHUNK ** -0.5),
        'b_s': gain(ks[9], (DEPTH, N_GROUPS_A, CHUNK)),
        'conv_w': nrm(ks[10], (DEPTH, CONV_W, D_B), CONV_W ** -0.5),
        'w_a': nrm(ks[11], (DEPTH, D_A, D_MODEL), D_A ** -0.5),
        'w_b': nrm(ks[12], (DEPTH, D_B, D_MODEL), D_B ** -0.5),
        'w_o': nrm(ks[13], (DEPTH, D_MODEL, D_MODEL), D_MODEL ** -0.5),
        'ln2': gain(ks[14], (DEPTH, D_MODEL)),
        'w_q': nrm(ks[15], (DEPTH, D_MODEL, PEER_HEADS * D_QUERY), D_MODEL ** -0.5),
        'k1': nrm(ks[16], (DEPTH, N_KEYS, D_HALF), D_HALF ** -0.5),
        'k2': nrm(ks[17], (DEPTH, N_KEYS, D_HALF), D_HALF ** -0.5),
        'u_tab': nrm(ks[18], (DEPTH, N_EXPERTS, D_MODEL), D_MODEL ** -0.5),
        'v_tab': nrm(ks[19], (DEPTH, N_EXPERTS, D_MODEL), PEER_HEADS ** -0.5),
        'ln3': gain(ks[20], (DEPTH, D_MODEL)),
        'w_pg': nrm(ks[21], (DEPTH, D_MODEL, D_MODEL), D_MODEL ** -0.5),
        'w_ple': nrm(ks[22], (DEPTH, D_PLE, D_MODEL), D_PLE ** -0.5),
        'ln_f': gain(ks[23], (D_MODEL,)),
    }


def reference(x_prompt, x_sample, state_conv, p_prompt, p_sample, ln1, w_in, ln_v,
              w_s, b_s, conv_w, w_a, w_b, w_o, ln2, w_q, k1, k2, u_tab, v_tab,
              ln3, w_pg, w_ple, ln_f):
    conv_p0 = jnp.zeros((x_prompt.shape[0], CONV_W - 1, D_B), x_prompt.dtype)
    xp, xs = x_prompt, x_sample
    conv_prompt, conv_sample, v_sample = [], [], []
    for i in range(DEPTH):
        lp = (ln1[i], w_in[i], ln_v[i], w_s[i], b_s[i], conv_w[i], w_a[i], w_b[i], w_o[i],
              ln2[i], w_q[i], k1[i], k2[i], u_tab[i], v_tab[i], ln3[i], w_pg[i], w_ple[i])
        xp, _, cp = _layer(xp, p_prompt[i], conv_p0, *lp)
        xs, vs, cs = _layer(xs, p_sample[i], state_conv[i], *lp)
        conv_prompt.append(cp)
        conv_sample.append(cs)
        v_sample.append(vs)
    y_prompt = _rmsnorm(xp, ln_f)
    y_sample = _rmsnorm(xs, ln_f)
    return (y_prompt, y_sample, jnp.stack(conv_prompt), jnp.stack(conv_sample), jnp.stack(v_sample))
```

```python
import functools

import jax
import jax.numpy as jnp
from jax import lax
from jax.experimental import pallas as pl
from jax.experimental.pallas import tpu as pltpu

F32 = jnp.float32
BF16 = jnp.bfloat16

D_MODEL = 1024
D_A = 1024
D_B = 1024
D_PLE = 256
CHUNK = 128
N_GROUPS = 8
GROUP = D_A // N_GROUPS
N_KEYS = 128
N_EXPERTS = N_KEYS * N_KEYS
HEADS = 8
D_QUERY = 256
D_HALF = 128
TOPK = 16
EPS = 1e-6
DEPTH = 2

N_PROMPT_SEQ = 8
PROMPT_LEN = 2048
N_SAMPLE_SEQ = 128
SAMPLE_LEN = 4
T_PROMPT = N_PROMPT_SEQ * PROMPT_LEN
T_SAMPLE = N_SAMPLE_SEQ * SAMPLE_LEN
T_ALL = T_PROMPT + T_SAMPLE

LANES = 128
SUBLANES = 8
MIX_ROWS = 256
SEL_TOKENS = 512
PEER_TOKENS = 512
PEER_EXPERTS = 1024
VMEM_LIMIT = 56 * 1024 * 1024
NEG = -1e30


def _rms(x, g):
    ms = jnp.mean(x * x, axis=-1, keepdims=True)
    return x * lax.rsqrt(ms + EPS) * g


def _sigmoid(x):
    return 1.0 / (1.0 + jnp.exp(-x))


def _dot(a, b):
    return jnp.dot(a, b, preferred_element_type=F32)


def _const_spec(shape, grid_rank):
    zeros = (0,) * len(shape)
    if grid_rank == 1:
        return pl.BlockSpec(shape, lambda i: zeros, pipeline_mode=pl.Buffered(1))
    return pl.BlockSpec(shape, lambda i, j: zeros, pipeline_mode=pl.Buffered(1))


def _mixer_front(x, ln1_ref, win_ref, lnv_ref, ws_ref, bias_ref, yin_ref, rows):
    h = _rms(x, ln1_ref[...]).astype(BF16)

    def proj(k):
        return _dot(h, win_ref[:, k * D_MODEL:(k + 1) * D_MODEL])

    u = jax.nn.gelu(proj(0))
    v = _rms(jax.nn.gelu(proj(1)), lnv_ref[...])
    vb = v.astype(BF16)
    r_i = lax.broadcasted_iota(jnp.int32, (CHUNK, CHUNK), 0)
    c_i = lax.broadcasted_iota(jnp.int32, (CHUNK, CHUNK), 1)
    causal = r_i >= c_i
    for g in range(N_GROUPS):
        w = jnp.where(causal, ws_ref[g], jnp.zeros((), BF16))
        cols = slice(g * GROUP, (g + 1) * GROUP)
        for c in range(rows // CHUNK):
            rws = slice(c * CHUNK, (c + 1) * CHUNK)
            s = _dot(w, vb[rws, cols]) + bias_ref[:, cols]
            yin_ref[rws, cols] = (u[rws, cols] * s).astype(BF16)
    return h, v, proj


def _mixer_back(x, proj, yc, wa_ref, wb_ref, wo_ref, yin_ref):
    ya = _dot(yin_ref[...], wa_ref[...])
    yb = _dot((proj(2) * yc).astype(BF16), wb_ref[...])
    mix = _sigmoid(proj(5)) * ya + _sigmoid(proj(6)) * yb
    return x + _dot(mix.astype(BF16), wo_ref[...])


def _mixer_prompt_kernel(x_ref, ln1_ref, win_ref, lnv_ref, ws_ref, bias_ref, cw_ref,
                         wa_ref, wb_ref, wo_ref, x1_ref, tail_ref, yin_ref, cbuf_ref):
    rows = MIX_ROWS
    x = x_ref[...]
    h, v, proj = _mixer_front(x, ln1_ref, win_ref, lnv_ref, ws_ref, bias_ref, yin_ref, rows)
    cx = proj(3) * proj(4)

    @pl.when(pl.program_id(1) == 0)
    def _():
        cbuf_ref[0:SUBLANES, :] = jnp.zeros((SUBLANES, D_B), F32)

    cbuf_ref[SUBLANES:SUBLANES + rows, :] = cx
    c1 = cbuf_ref[SUBLANES - 1:SUBLANES - 1 + rows, :]
    c2 = cbuf_ref[SUBLANES - 2:SUBLANES - 2 + rows, :]
    yc = cw_ref[0:1, :] * c2 + cw_ref[1:2, :] * c1 + cw_ref[2:3, :] * cx
    cbuf_ref[0:SUBLANES, :] = cx[rows - SUBLANES:rows, :]
    tail_ref[0] = cx[rows - 2:rows, :]
    x1_ref[...] = _mixer_back(x, proj, yc, wa_ref, wb_ref, wo_ref, yin_ref)


def _mixer_sample_kernel(x_ref, ln1_ref, win_ref, lnv_ref, ws_ref, bias_ref, cw_ref,
                         wa_ref, wb_ref, wo_ref, p1_ref, p2_ref,
                         x1_ref, v_ref, cx_ref, yin_ref, cbuf_ref):
    rows = T_SAMPLE
    x = x_ref[...]
    h, v, proj = _mixer_front(x, ln1_ref, win_ref, lnv_ref, ws_ref, bias_ref, yin_ref, rows)
    v_ref[...] = v
    cx = proj(3) * proj(4)
    cx_ref[...] = cx
    cbuf_ref[0:SUBLANES, :] = jnp.zeros((SUBLANES, D_B), F32)
    cbuf_ref[SUBLANES:SUBLANES + rows, :] = cx
    pos = lax.broadcasted_iota(jnp.int32, (rows, D_B), 0) & (SAMPLE_LEN - 1)
    c1 = jnp.where(pos >= 1, cbuf_ref[SUBLANES - 1:SUBLANES - 1 + rows, :], p1_ref[...])
    c2 = jnp.where(pos >= 2, cbuf_ref[SUBLANES - 2:SUBLANES - 2 + rows, :], p2_ref[...])
    yc = cw_ref[0:1, :] * c2 + cw_ref[1:2, :] * c1 + cw_ref[2:3, :] * cx
    x1_ref[...] = _mixer_back(x, proj, yc, wa_ref, wb_ref, wo_ref, yin_ref)


def _mixer_weight_specs(grid_rank):
    cs = functools.partial(_const_spec, grid_rank=grid_rank)
    return [
        cs((1, D_MODEL)),
        cs((D_MODEL, 7 * D_MODEL)),
        cs((1, D_A)),
        cs((N_GROUPS, CHUNK, CHUNK)),
        cs((CHUNK, D_A)),
        cs((3, D_B)),
        cs((D_A, D_MODEL)),
        cs((D_B, D_MODEL)),
        cs((D_MODEL, D_MODEL)),
    ]


def _mixer_prompt(x2d, weights):
    nj = PROMPT_LEN // MIX_ROWS
    return pl.pallas_call(
        _mixer_prompt_kernel,
        grid=(N_PROMPT_SEQ, nj),
        in_specs=[pl.BlockSpec((MIX_ROWS, D_MODEL), lambda b, j: (b * nj + j, 0))]
        + _mixer_weight_specs(2),
        out_specs=[
            pl.BlockSpec((MIX_ROWS, D_MODEL), lambda b, j: (b * nj + j, 0)),
            pl.BlockSpec((1, 2, D_B), lambda b, j: (b, 0, 0)),
        ],
        out_shape=[
            jax.ShapeDtypeStruct((T_PROMPT, D_MODEL), F32),
            jax.ShapeDtypeStruct((N_PROMPT_SEQ, 2, D_B), F32),
        ],
        scratch_shapes=[
            pltpu.VMEM((MIX_ROWS, D_A), BF16),
            pltpu.VMEM((MIX_ROWS + SUBLANES, D_B), F32),
        ],
        compiler_params=pltpu.CompilerParams(
            dimension_semantics=("arbitrary", "arbitrary"), vmem_limit_bytes=VMEM_LIMIT),
        name="mixer_prompt",
    )(x2d, *weights)


def _mixer_sample(x2d, row_block, weights, p1, p2):
    full = lambda shape: pl.BlockSpec(shape, lambda i: (0,) * len(shape))
    return pl.pallas_call(
        _mixer_sample_kernel,
        grid=(1,),
        in_specs=[pl.BlockSpec((T_SAMPLE, D_MODEL), lambda i: (row_block, 0))]
        + _mixer_weight_specs(1)
        + [full((T_SAMPLE, D_B)), full((T_SAMPLE, D_B))],
        out_specs=[full((T_SAMPLE, D_MODEL)), full((T_SAMPLE, D_A)), full((T_SAMPLE, D_B))],
        out_shape=[
            jax.ShapeDtypeStruct((T_SAMPLE, D_MODEL), F32),
            jax.ShapeDtypeStruct((T_SAMPLE, D_A), F32),
            jax.ShapeDtypeStruct((T_SAMPLE, D_B), F32),
        ],
        scratch_shapes=[
            pltpu.VMEM((T_SAMPLE, D_A), BF16),
            pltpu.VMEM((T_SAMPLE + SUBLANES, D_B), F32),
        ],
        compiler_params=pltpu.CompilerParams(
            dimension_semantics=("arbitrary",), vmem_limit_bytes=VMEM_LIMIT),
        name="mixer_sample",
    )(x2d, *weights, p1, p2)


def _cmpx(v, i, j):
    hi = jnp.maximum(v[i], v[j])
    lo = jnp.minimum(v[i], v[j])
    v[i] = hi
    v[j] = lo


def _bitonic_merge16(v):
    j = TOPK // 2
    while j >= 1:
        for i in range(TOPK):
            l = i ^ j
            if l > i:
                _cmpx(v, i, l)
        j //= 2


def _sort16(v):
    k = 2
    while k <= TOPK:
        j = k // 2
        while j >= 1:
            for i in range(TOPK):
                l = i ^ j
                if l > i:
                    if (i & k) == 0:
                        _cmpx(v, i, l)
                    else:
                        _cmpx(v, l, i)
            j //= 2
        k *= 2


def _top16(s):
    v = [s[k] for k in range(TOPK)]
    _sort16(v)
    for shift in (4, 2, 1):
        w = [pltpu.roll(v[k], shift, 0) for k in range(TOPK)]
        v = [jnp.maximum(v[k], w[TOPK - 1 - k]) for k in range(TOPK)]
        _bitonic_merge16(v)
    return v


def _sublane_allreduce(x, op):
    for shift in (4, 2, 1):
        x = op(x, pltpu.roll(x, shift, 0))
    return x


def _on_sublanes(vals, sub):
    out = vals[SUBLANES - 1]
    for r in range(SUBLANES - 2, -1, -1):
        out = jnp.where(sub == r, vals[r], out)
    return out


def _select_kernel(x1_ref, ln2_ref, wqt_ref, k1_ref, k2_ref,
                   h2t_ref, s2_ref, e2_ref, th_ref, p1_ref, s1_ref):
    tokens = SEL_TOKENS
    h2 = _rms(x1_ref[...], ln2_ref[...])
    h2t = h2.T.astype(BF16)
    h2t_ref[...] = h2t
    qt = _dot(wqt_ref[...], h2t)
    for h in range(HEADS):
        q1 = qt[h * D_QUERY:h * D_QUERY + D_HALF].astype(BF16)
        q2 = qt[h * D_QUERY + D_HALF:(h + 1) * D_QUERY].astype(BF16)
        s1_ref[h] = _dot(k1_ref[...], q1).reshape(TOPK, SUBLANES, tokens)
        s2_ref[h] = _dot(k2_ref[...], q2).reshape(TOPK, SUBLANES, tokens)

    sub = lax.broadcasted_iota(jnp.int32, (SUBLANES, LANES), 0)

    def lane_block(lb, carry):
        lanes = pl.ds(pl.multiple_of(lb * LANES, LANES), LANES)
        for h in range(HEADS):
            s1 = s1_ref[h, :, :, lanes]
            s2 = s2_ref[h, :, :, lanes]
            t1 = _top16(s1)
            t2 = _top16(s2)
            t2lo = _on_sublanes(t2[0:8], sub)
            t2hi = _on_sublanes(t2[8:16], sub)
            t1hi = _on_sublanes(t1[8:16], sub)
            cands = [t1[0] + t2lo, t1[0] + t2hi, t1[1] + t2lo]
            for a, nb in ((2, 5), (3, 4), (4, 3), (5, 2), (6, 2), (7, 2)):
                cands.append(jnp.where(sub < nb, t1[a] + t2lo, NEG))
            cands.append(t1hi + t2[0])
            cur = list(cands)
            tops = []
            for it in range(TOPK + 1):
                m = functools.reduce(jnp.maximum, cur)
                m = _sublane_allreduce(m, jnp.maximum)
                tops.append(m)
                if it < TOPK:
                    cur = [jnp.where(c == m, NEG, c) for c in cur]
            tau = 0.5 * (tops[TOPK - 1] + tops[TOPK])
            z = functools.reduce(
                jnp.add, [jnp.where(c >= tau, jnp.exp(c - tops[0]), 0.0) for c in cands])
            z = _sublane_allreduce(z, jnp.add)
            inv_z = 1.0 / z
            p1 = jnp.where(s1 >= t1[TOPK - 1][None], jnp.exp(s1 - t1[0][None]) * inv_z[None], 0.0)
            e2 = jnp.where(s2 >= t2[TOPK - 1][None], jnp.exp(s2 - t2[0][None]), 0.0)
            p1_ref[h, :, :, lanes] = p1
            e2_ref[h, :, :, lanes] = e2
            th_ref[h, :, :, lanes] = tau[None] - s1
        return carry

    lax.fori_loop(0, tokens // LANES, lane_block, 0)


def _select(x1, ln2, wqt, k1, k2):
    nb = T_ALL // SEL_TOKENS
    key_shape = (HEADS, TOPK, SUBLANES, T_ALL)
    key_spec = pl.BlockSpec((HEADS, TOPK, SUBLANES, SEL_TOKENS), lambda i: (0, 0, 0, i))
    return pl.pallas_call(
        _select_kernel,
        grid=(nb,),
        in_specs=[
            pl.BlockSpec((SEL_TOKENS, D_MODEL), lambda i: (i, 0)),
            _const_spec((1, D_MODEL), 1),
            _const_spec((HEADS * D_QUERY, D_MODEL), 1),
            _const_spec((N_KEYS, D_HALF), 1),
            _const_spec((N_KEYS, D_HALF), 1),
        ],
        out_specs=[pl.BlockSpec((D_MODEL, SEL_TOKENS), lambda i: (0, i))] + [key_spec] * 4,
        out_shape=[jax.ShapeDtypeStruct((D_MODEL, T_ALL), BF16)]
        + [jax.ShapeDtypeStruct(key_shape, F32)] * 4,
        scratch_shapes=[pltpu.VMEM((HEADS, TOPK, SUBLANES, SEL_TOKENS), F32)],
        compiler_params=pltpu.CompilerParams(
            dimension_semantics=("arbitrary",), vmem_limit_bytes=VMEM_LIMIT),
        name="peer_select",
    )(x1, ln2, wqt, k1, k2)


def _peer_kernel(final, h2t_ref, s2_ref, e2_ref, th_ref, p1_ref, u_ref, vt_ref, x1_ref, pe_ref,
                 ln3_ref, wpg_ref, wple_ref, lnf_ref, o_ref, acc_ref, at_ref, ga_ref):
    et = pl.program_id(1)
    half = N_KEYS // 2

    @pl.when(et == 0)
    def _():
        acc_ref[...] = jnp.zeros_like(acc_ref)

    at_ref[...] = _dot(u_ref[...], h2t_ref[...])

    def lane_block(lb, carry):
        lanes = pl.ds(pl.multiple_of(lb * LANES, LANES), LANES)
        for r in range(PEER_EXPERTS // N_KEYS):
            for jh in range(2):
                keys = slice(jh * half, (jh + 1) * half)
                g = jnp.zeros((half, LANES), F32)
                for h in range(HEADS):
                    th = th_ref[h, r:r + 1, lanes]
                    p = p1_ref[h, r:r + 1, lanes]
                    g = g + jnp.where(s2_ref[h, keys, lanes] >= th, e2_ref[h, keys, lanes] * p, 0.0)
                rows = slice(r * N_KEYS + jh * half, r * N_KEYS + (jh + 1) * half)
                a = jax.nn.gelu(at_ref[rows, lanes])
                ga_ref[rows, lanes] = (g * a).astype(BF16)
        return carry

    lax.fori_loop(0, PEER_TOKENS // LANES, lane_block, 0)
    acc_ref[...] += _dot(vt_ref[...], ga_ref[...])

    @pl.when(et == pl.num_programs(1) - 1)
    def _():
        x2 = x1_ref[...] + acc_ref[...].T
        h3 = _rms(x2, ln3_ref[...]).astype(BF16)
        gate = _sigmoid(_dot(h3, wpg_ref[...]))
        x3 = x2 + gate * _dot(pe_ref[...].astype(BF16), wple_ref[...])
        o_ref[...] = _rms(x3, lnf_ref[...]) if final else x3


def _peer(final, h2t, s2, e2, th, p1, u_bf, vt_bf, x1, pe, ln3, wpg, wple, lnf):
    nt = T_ALL // PEER_TOKENS
    ne = N_EXPERTS // PEER_EXPERTS
    rows_per_step = PEER_EXPERTS // N_KEYS
    tok_spec = lambda width: pl.BlockSpec((PEER_TOKENS, width), lambda t, e: (t, 0))
    key_spec = pl.BlockSpec((HEADS, N_KEYS, PEER_TOKENS), lambda t, e: (0, 0, t))
    row_spec = pl.BlockSpec((HEADS, None, rows_per_step, PEER_TOKENS), lambda t, e: (0, e, 0, t))
    return pl.pallas_call(
        functools.partial(_peer_kernel, final),
        grid=(nt, ne),
        in_specs=[
            pl.BlockSpec((D_MODEL, PEER_TOKENS), lambda t, e: (0, t)),
            key_spec, key_spec, row_spec, row_spec,
            pl.BlockSpec((PEER_EXPERTS, D_MODEL), lambda t, e: (e, 0)),
            pl.BlockSpec((D_MODEL, PEER_EXPERTS), lambda t, e: (0, e)),
            tok_spec(D_MODEL), tok_spec(D_PLE),
            _const_spec((1, D_MODEL), 2),
            _const_spec((D_MODEL, D_MODEL), 2),
            _const_spec((D_PLE, D_MODEL), 2),
            _const_spec((1, D_MODEL), 2),
        ],
        out_specs=tok_spec(D_MODEL),
        out_shape=jax.ShapeDtypeStruct((T_ALL, D_MODEL), F32),
        scratch_shapes=[
            pltpu.VMEM((D_MODEL, PEER_TOKENS), F32),
            pltpu.VMEM((PEER_EXPERTS, PEER_TOKENS), F32),
            pltpu.VMEM((PEER_EXPERTS, PEER_TOKENS), BF16),
        ],
        compiler_params=pltpu.CompilerParams(
            dimension_semantics=("arbitrary", "arbitrary"), vmem_limit_bytes=VMEM_LIMIT),
        name="peer_dense",
    )(h2t, s2, e2, th, p1, u_bf, vt_bf, x1, pe, ln3, wpg, wple, lnf)


def kernel(x_prompt, x_sample, state_conv, p_prompt, p_sample, ln1, w_in, ln_v, w_s, b_s, conv_w,
           w_a, w_b, w_o, ln2, w_q, k1, k2, u_tab, v_tab, ln3, w_pg, w_ple, ln_f):
    assert x_prompt.shape == (N_PROMPT_SEQ, PROMPT_LEN, D_MODEL)
    assert x_sample.shape == (N_SAMPLE_SEQ, SAMPLE_LEN, D_MODEL)
    assert u_tab.shape == (DEPTH, N_EXPERTS, D_MODEL)

    x_p = x_prompt.reshape(T_PROMPT, D_MODEL)
    x_s = x_sample.reshape(T_SAMPLE, D_MODEL)
    sample_block = 0
    seqs_per_chunk = CHUNK // SAMPLE_LEN
    eye = jnp.eye(seqs_per_chunk, dtype=F32)
    lnf = ln_f.reshape(1, D_MODEL)
    conv_prompt, conv_sample, v_sample = [], [], []
    y = None

    for i in range(DEPTH):
        row = lambda a: a[i].reshape(1, -1)
        ws_p = w_s[i].astype(BF16)
        corner = w_s[i][:, :SAMPLE_LEN, :SAMPLE_LEN]
        ws_s = (eye[None, :, None, :, None] * corner[:, None, :, None, :]).reshape(
            N_GROUPS, CHUNK, CHUNK).astype(BF16)
        bias_p = jnp.repeat(b_s[i].T, GROUP, axis=1)
        bias_s = jnp.tile(jnp.repeat(b_s[i][:, :SAMPLE_LEN].T, GROUP, axis=1), (seqs_per_chunk, 1))
        shared = (conv_w[i], w_a[i].astype(BF16), w_b[i].astype(BF16), w_o[i].astype(BF16))
        front = (row(ln1), w_in[i].astype(BF16), row(ln_v))
        w_prompt = front + (ws_p, bias_p) + shared
        w_sample = front + (ws_s, bias_s) + shared

        prev = state_conv[i]
        zero = jnp.zeros((N_SAMPLE_SEQ, 1, D_B), F32)
        tap1 = jnp.concatenate([prev[:, 1:2], zero, zero, zero], axis=1).reshape(T_SAMPLE, D_B)
        tap2 = jnp.concatenate([prev[:, 0:1], prev[:, 1:2], zero, zero], axis=1).reshape(T_SAMPLE, D_B)

        x1_p, tail_p = _mixer_prompt(x_p, w_prompt)
        x1_s, v_s, cx_s = _mixer_sample(x_s, sample_block, w_sample, tap1, tap2)
        conv_prompt.append(tail_p)
        conv_sample.append(cx_s.reshape(N_SAMPLE_SEQ, SAMPLE_LEN, D_B)[:, SAMPLE_LEN - 2:])
        v_sample.append(v_s.reshape(N_SAMPLE_SEQ, SAMPLE_LEN, D_A))

        x1 = jnp.concatenate([x1_p, x1_s], axis=0)
        pe = jnp.concatenate([p_prompt[i].reshape(T_PROMPT, D_PLE),
                              p_sample[i].reshape(T_SAMPLE, D_PLE)], axis=0)

        h2t, s2, e2, th, p1 = _select(x1, row(ln2), w_q[i].T.astype(BF16),
                                      k1[i].astype(BF16), k2[i].astype(BF16))
        s2 = s2.reshape(HEADS, N_KEYS, T_ALL)
        e2 = e2.reshape(HEADS, N_KEYS, T_ALL)
        final = i == DEPTH - 1
        y = _peer(final, h2t, s2, e2, th, p1, u_tab[i].astype(BF16), v_tab[i].T.astype(BF16),
                  x1, pe, row(ln3), w_pg[i].astype(BF16), w_ple[i].astype(BF16), lnf)
        x_p = y
        x_s = y
        sample_block = T_PROMPT // T_SAMPLE

    y_prompt = y[:T_PROMPT].reshape(N_PROMPT_SEQ, PROMPT_LEN, D_MODEL)
    y_sample = y[T_PROMPT:].reshape(N_SAMPLE_SEQ, SAMPLE_LEN, D_MODEL)
    return (y_prompt, y_sample, jnp.stack(conv_prompt), jnp.stack(conv_sample), jnp.stack(v_sample))
```

```python
import functools

import jax
import jax.numpy as jnp
from jax import lax
from jax.experimental import pallas as pl
from jax.experimental.pallas import tpu as pltpu

F32 = jnp.float32
BF16 = jnp.bfloat16

D_MODEL = 1024
D_A = 1024
D_B = 1024
D_PLE = 256
CHUNK = 128
N_GROUPS = 8
GROUP = D_A // N_GROUPS
N_KEYS = 128
N_EXPERTS = N_KEYS * N_KEYS
HEADS = 8
D_QUERY = 256
D_HALF = 128
TOPK = 16
EPS = 1e-6
DEPTH = 2

N_PROMPT_SEQ = 8
PROMPT_LEN = 2048
N_SAMPLE_SEQ = 128
SAMPLE_LEN = 4
T_PROMPT = N_PROMPT_SEQ * PROMPT_LEN
T_SAMPLE = N_SAMPLE_SEQ * SAMPLE_LEN
T_ALL = T_PROMPT + T_SAMPLE

LANES = 128
SUBLANES = 8
MIX_ROWS = 256
SEL_TOKENS = 512
PEER_TOKENS = 512
PEER_EXPERTS = 1024
PEER_CHUNKS = 4
CHUNK_EXPERTS = PEER_EXPERTS // PEER_CHUNKS
BF16_ROWS = 16
GATE_KEYS = 64
VMEM_LIMIT = 56 * 1024 * 1024
NEG = -1e30


def _rms(x, g):
    ms = jnp.mean(x * x, axis=-1, keepdims=True)
    return x * lax.rsqrt(ms + EPS) * g


def _sigmoid(x):
    return 1.0 / (1.0 + jnp.exp(-x))


def _dot(a, b):
    return jnp.dot(a, b, preferred_element_type=F32)


def _const_spec(shape, grid_rank):
    zeros = (0,) * len(shape)
    if grid_rank == 1:
        return pl.BlockSpec(shape, lambda i: zeros, pipeline_mode=pl.Buffered(1))
    return pl.BlockSpec(shape, lambda i, j: zeros, pipeline_mode=pl.Buffered(1))


def _mixer_front(x, ln1_ref, win_ref, lnv_ref, ws_ref, bias_ref, yin_ref, rows):
    h = _rms(x, ln1_ref[...]).astype(BF16)

    def proj(k):
        return _dot(h, win_ref[:, k * D_MODEL:(k + 1) * D_MODEL])

    u = jax.nn.gelu(proj(0))
    v = _rms(jax.nn.gelu(proj(1)), lnv_ref[...])
    vb = v.astype(BF16)
    r_i = lax.broadcasted_iota(jnp.int32, (CHUNK, CHUNK), 0)
    c_i = lax.broadcasted_iota(jnp.int32, (CHUNK, CHUNK), 1)
    causal = r_i >= c_i
    for g in range(N_GROUPS):
        w = jnp.where(causal, ws_ref[g], jnp.zeros((), BF16))
        cols = slice(g * GROUP, (g + 1) * GROUP)
        for c in range(rows // CHUNK):
            rws = slice(c * CHUNK, (c + 1) * CHUNK)
            s = _dot(w, vb[rws, cols]) + bias_ref[:, cols]
            yin_ref[rws, cols] = (u[rws, cols] * s).astype(BF16)
    return h, v, proj


def _mixer_back(x, proj, yc, wa_ref, wb_ref, wo_ref, yin_ref):
    ya = _dot(yin_ref[...], wa_ref[...])
    yb = _dot((proj(2) * yc).astype(BF16), wb_ref[...])
    mix = _sigmoid(proj(5)) * ya + _sigmoid(proj(6)) * yb
    return x + _dot(mix.astype(BF16), wo_ref[...])


def _mixer_prompt_kernel(x_ref, ln1_ref, win_ref, lnv_ref, ws_ref, bias_ref, cw_ref,
                         wa_ref, wb_ref, wo_ref, x1_ref, tail_ref, yin_ref, cbuf_ref):
    rows = MIX_ROWS
    x = x_ref[...]
    h, v, proj = _mixer_front(x, ln1_ref, win_ref, lnv_ref, ws_ref, bias_ref, yin_ref, rows)
    cx = proj(3) * proj(4)

    @pl.when(pl.program_id(1) == 0)
    def _():
        cbuf_ref[0:SUBLANES, :] = jnp.zeros((SUBLANES, D_B), F32)

    cbuf_ref[SUBLANES:SUBLANES + rows, :] = cx
    c1 = cbuf_ref[SUBLANES - 1:SUBLANES - 1 + rows, :]
    c2 = cbuf_ref[SUBLANES - 2:SUBLANES - 2 + rows, :]
    yc = cw_ref[0:1, :] * c2 + cw_ref[1:2, :] * c1 + cw_ref[2:3, :] * cx
    cbuf_ref[0:SUBLANES, :] = cx[rows - SUBLANES:rows, :]
    tail_ref[0] = cx[rows - 2:rows, :]
    x1_ref[...] = _mixer_back(x, proj, yc, wa_ref, wb_ref, wo_ref, yin_ref)


def _mixer_sample_kernel(x_ref, ln1_ref, win_ref, lnv_ref, ws_ref, bias_ref, cw_ref,
                         wa_ref, wb_ref, wo_ref, p1_ref, p2_ref,
                         x1_ref, v_ref, cx_ref, yin_ref, cbuf_ref):
    rows = T_SAMPLE
    x = x_ref[...]
    h, v, proj = _mixer_front(x, ln1_ref, win_ref, lnv_ref, ws_ref, bias_ref, yin_ref, rows)
    v_ref[...] = v
    cx = proj(3) * proj(4)
    cx_ref[...] = cx
    cbuf_ref[0:SUBLANES, :] = jnp.zeros((SUBLANES, D_B), F32)
    cbuf_ref[SUBLANES:SUBLANES + rows, :] = cx
    pos = lax.broadcasted_iota(jnp.int32, (rows, D_B), 0) & (SAMPLE_LEN - 1)
    c1 = jnp.where(pos >= 1, cbuf_ref[SUBLANES - 1:SUBLANES - 1 + rows, :], p1_ref[...])
    c2 = jnp.where(pos >= 2, cbuf_ref[SUBLANES - 2:SUBLANES - 2 + rows, :], p2_ref[...])
    yc = cw_ref[0:1, :] * c2 + cw_ref[1:2, :] * c1 + cw_ref[2:3, :] * cx
    x1_ref[...] = _mixer_back(x, proj, yc, wa_ref, wb_ref, wo_ref, yin_ref)


def _mixer_weight_specs(grid_rank):
    cs = functools.partial(_const_spec, grid_rank=grid_rank)
    return [
        cs((1, D_MODEL)),
        cs((D_MODEL, 7 * D_MODEL)),
        cs((1, D_A)),
        cs((N_GROUPS, CHUNK, CHUNK)),
        cs((CHUNK, D_A)),
        cs((3, D_B)),
        cs((D_A, D_MODEL)),
        cs((D_B, D_MODEL)),
        cs((D_MODEL, D_MODEL)),
    ]


def _mixer_prompt(x2d, weights):
    nj = PROMPT_LEN // MIX_ROWS
    return pl.pallas_call(
        _mixer_prompt_kernel,
        grid=(N_PROMPT_SEQ, nj),
        in_specs=[pl.BlockSpec((MIX_ROWS, D_MODEL), lambda b, j: (b * nj + j, 0))]
        + _mixer_weight_specs(2),
        out_specs=[
            pl.BlockSpec((MIX_ROWS, D_MODEL), lambda b, j: (b * nj + j, 0)),
            pl.BlockSpec((1, 2, D_B), lambda b, j: (b, 0, 0)),
        ],
        out_shape=[
            jax.ShapeDtypeStruct((T_PROMPT, D_MODEL), F32),
            jax.ShapeDtypeStruct((N_PROMPT_SEQ, 2, D_B), F32),
        ],
        scratch_shapes=[
            pltpu.VMEM((MIX_ROWS, D_A), BF16),
            pltpu.VMEM((MIX_ROWS + SUBLANES, D_B), F32),
        ],
        compiler_params=pltpu.CompilerParams(
            dimension_semantics=("arbitrary", "arbitrary"), vmem_limit_bytes=VMEM_LIMIT),
        name="mixer_prompt",
    )(x2d, *weights)


def _mixer_sample(x2d, row_block, weights, p1, p2):
    full = lambda shape: pl.BlockSpec(shape, lambda i: (0,) * len(shape))
    return pl.pallas_call(
        _mixer_sample_kernel,
        grid=(1,),
        in_specs=[pl.BlockSpec((T_SAMPLE, D_MODEL), lambda i: (row_block, 0))]
        + _mixer_weight_specs(1)
        + [full((T_SAMPLE, D_B)), full((T_SAMPLE, D_B))],
        out_specs=[full((T_SAMPLE, D_MODEL)), full((T_SAMPLE, D_A)), full((T_SAMPLE, D_B))],
        out_shape=[
            jax.ShapeDtypeStruct((T_SAMPLE, D_MODEL), F32),
            jax.ShapeDtypeStruct((T_SAMPLE, D_A), F32),
            jax.ShapeDtypeStruct((T_SAMPLE, D_B), F32),
        ],
        scratch_shapes=[
            pltpu.VMEM((T_SAMPLE, D_A), BF16),
            pltpu.VMEM((T_SAMPLE + SUBLANES, D_B), F32),
        ],
        compiler_params=pltpu.CompilerParams(
            dimension_semantics=("arbitrary",), vmem_limit_bytes=VMEM_LIMIT),
        name="mixer_sample",
    )(x2d, *weights, p1, p2)


def _cmpx(v, i, j):
    hi = jnp.maximum(v[i], v[j])
    lo = jnp.minimum(v[i], v[j])
    v[i] = hi
    v[j] = lo


def _bitonic_merge16(v):
    j = TOPK // 2
    while j >= 1:
        for i in range(TOPK):
            l = i ^ j
            if l > i:
                _cmpx(v, i, l)
        j //= 2


def _sort16(v):
    k = 2
    while k <= TOPK:
        j = k // 2
        while j >= 1:
            for i in range(TOPK):
                l = i ^ j
                if l > i:
                    if (i & k) == 0:
                        _cmpx(v, i, l)
                    else:
                        _cmpx(v, l, i)
            j //= 2
        k *= 2


def _top16(s):
    v = [s[k] for k in range(TOPK)]
    _sort16(v)
    for shift in (4, 2, 1):
        w = [pltpu.roll(v[k], shift, 0) for k in range(TOPK)]
        v = [jnp.maximum(v[k], w[TOPK - 1 - k]) for k in range(TOPK)]
        _bitonic_merge16(v)
    return v


def _sublane_allreduce(x, op):
    for shift in (4, 2, 1):
        x = op(x, pltpu.roll(x, shift, 0))
    return x


def _on_sublanes(vals, sub):
    out = vals[SUBLANES - 1]
    for r in range(SUBLANES - 2, -1, -1):
        out = jnp.where(sub == r, vals[r], out)
    return out


def _select_kernel(x1_ref, ln2_ref, wqt_ref, k1_ref, k2_ref,
                   h2t_ref, r2_ref, e2_ref, c1_ref, p1_ref, s1_scr, s2_scr):
    tokens = SEL_TOKENS
    h2 = _rms(x1_ref[...], ln2_ref[...])
    h2t = h2.T.astype(BF16)
    h2t_ref[...] = h2t
    qt = _dot(wqt_ref[...], h2t)
    for h in range(HEADS):
        q1 = qt[h * D_QUERY:h * D_QUERY + D_HALF].astype(BF16)
        q2 = qt[h * D_QUERY + D_HALF:(h + 1) * D_QUERY].astype(BF16)
        s1_scr[h] = _dot(k1_ref[...], q1).reshape(TOPK, SUBLANES, tokens)
        s2_scr[h] = _dot(k2_ref[...], q2).reshape(TOPK, SUBLANES, tokens)

    sub = lax.broadcasted_iota(jnp.int32, (SUBLANES, LANES), 0)

    def lane_block(lb, carry):
        lanes = pl.ds(pl.multiple_of(lb * LANES, LANES), LANES)
        for h in range(HEADS):
            s1 = s1_scr[h, :, :, lanes]
            s2 = s2_scr[h, :, :, lanes]
            t1 = _top16(s1)
            t2 = _top16(s2)
            t2lo = _on_sublanes(t2[0:8], sub)
            t2hi = _on_sublanes(t2[8:16], sub)
            t1hi = _on_sublanes(t1[8:16], sub)
            cands = [t1[0] + t2lo, t1[0] + t2hi, t1[1] + t2lo]
            for a, nb in ((2, 5), (3, 4), (4, 3), (5, 2), (6, 2), (7, 2)):
                cands.append(jnp.where(sub < nb, t1[a] + t2lo, NEG))
            cands.append(t1hi + t2[0])
            cur = list(cands)
            tops = []
            for it in range(TOPK + 1):
                m = functools.reduce(jnp.maximum, cur)
                m = _sublane_allreduce(m, jnp.maximum)
                tops.append(m)
                if it < TOPK:
                    cur = [jnp.where(c == m, NEG, c) for c in cur]
            tau = 0.5 * (tops[TOPK - 1] + tops[TOPK])
            z = functools.reduce(
                jnp.add, [jnp.where(c >= tau, jnp.exp(c - tops[0]), 0.0) for c in cands])
            z = _sublane_allreduce(z, jnp.add)
            inv_z = 1.0 / z
            p1 = jnp.where(s1 >= t1[TOPK - 1][None], jnp.exp(s1 - t1[0][None]) * inv_z[None], 0.0)
            e2 = jnp.where(s2 >= t2[TOPK - 1][None], jnp.exp(s2 - t2[0][None]), 0.0)
            th = tau[None] - s1
            rank2 = jnp.zeros_like(s2)
            count1 = jnp.zeros_like(s1)
            for b in range(TOPK):
                rank2 = rank2 + jnp.where(t2[b][None] > s2, 1.0, 0.0)
                count1 = count1 + jnp.where(t2[b][None] >= th, 1.0, 0.0)
            r2_ref[h, lb] = rank2
            e2_ref[h, lb] = e2
            c1_ref[h, lb] = count1
            p1_ref[h, lb] = p1
        return carry

    lax.fori_loop(0, tokens // LANES, lane_block, 0)


def _select(x1, ln2, wqt, k1, k2):
    nb = T_ALL // SEL_TOKENS
    lb_per_step = SEL_TOKENS // LANES
    n_lb = T_ALL // LANES
    key_shape = jax.ShapeDtypeStruct((HEADS, n_lb, TOPK, SUBLANES, LANES), F32)
    key_spec = pl.BlockSpec((HEADS, lb_per_step, TOPK, SUBLANES, LANES), lambda i: (0, i, 0, 0, 0))
    scr = pltpu.VMEM((HEADS, TOPK, SUBLANES, SEL_TOKENS), F32)
    return pl.pallas_call(
        _select_kernel,
        grid=(nb,),
        in_specs=[
            pl.BlockSpec((SEL_TOKENS, D_MODEL), lambda i: (i, 0)),
            _const_spec((1, D_MODEL), 1),
            _const_spec((HEADS * D_QUERY, D_MODEL), 1),
            _const_spec((N_KEYS, D_HALF), 1),
            _const_spec((N_KEYS, D_HALF), 1),
        ],
        out_specs=[pl.BlockSpec((D_MODEL, SEL_TOKENS), lambda i: (0, i))] + [key_spec] * 4,
        out_shape=[jax.ShapeDtypeStruct((D_MODEL, T_ALL), BF16)] + [key_shape] * 4,
        scratch_shapes=[scr, scr],
        compiler_params=pltpu.CompilerParams(
            dimension_semantics=("arbitrary",), vmem_limit_bytes=VMEM_LIMIT),
        name="peer_select",
    )(x1, ln2, wqt, k1, k2)


def _peer_kernel(final, h2t_ref, r2_ref, e2_ref, c1_ref, p1_ref, u_ref, vt_ref, x1_ref, pe_ref,
                 ln3_ref, wpg_ref, wple_ref, lnf_ref, o_ref, acc_ref, r2_scr, e2_scr, *chunk_refs):
    et = pl.program_id(1)
    at_refs = chunk_refs[:PEER_CHUNKS]
    ga_refs = chunk_refs[PEER_CHUNKS:]
    rows_per_chunk = CHUNK_EXPERTS // N_KEYS
    vregs = GATE_KEYS // BF16_ROWS

    @pl.when(et == 0)
    def _():
        acc_ref[...] = jnp.zeros_like(acc_ref)
        for h in range(HEADS):
            r2_scr[h] = r2_ref[h].astype(BF16)
            e2_scr[h] = e2_ref[h].astype(BF16)

    def lane_bcast(ref, h, lb, r):
        return jnp.broadcast_to(ref[h, lb, r:r + 1, :], (BF16_ROWS, LANES)).astype(BF16)

    def gate_tile(q, lb, kq):
        lanes = slice(lb * LANES, (lb + 1) * LANES)
        keys = [slice(kq * GATE_KEYS + k * BF16_ROWS, kq * GATE_KEYS + (k + 1) * BF16_ROWS)
                for k in range(vregs)]
        g = [[None] * vregs for _ in range(rows_per_chunk)]
        for h in range(HEADS):
            r2 = [r2_scr[h, lb, ks, :] for ks in keys]
            e2 = [e2_scr[h, lb, ks, :] for ks in keys]
            for r in range(rows_per_chunk):
                c = lane_bcast(c1_ref, h, lb, q * rows_per_chunk + r)
                p = lane_bcast(p1_ref, h, lb, q * rows_per_chunk + r)
                for k in range(vregs):
                    term = jnp.where(r2[k] < c, e2[k], jnp.zeros_like(e2[k])) * p
                    g[r][k] = term if h == 0 else g[r][k] + term
        for r in range(rows_per_chunk):
            for k in range(vregs):
                rows = slice(r * N_KEYS + keys[k].start, r * N_KEYS + keys[k].stop)
                a = at_refs[q][rows, lanes].astype(BF16)
                ga_refs[q][rows, lanes] = g[r][k] * jax.nn.gelu(a)

    def project(q):
        rows = slice(q * CHUNK_EXPERTS, (q + 1) * CHUNK_EXPERTS)
        at_refs[q][...] = _dot(u_ref[rows, :], h2t_ref[...])

    project(0)
    for q in range(PEER_CHUNKS):
        if q + 1 < PEER_CHUNKS:
            project(q + 1)
        for lb in range(PEER_TOKENS // LANES):
            for kq in range(N_KEYS // GATE_KEYS):
                gate_tile(q, lb, kq)
        cols = slice(q * CHUNK_EXPERTS, (q + 1) * CHUNK_EXPERTS)
        acc_ref[...] += _dot(vt_ref[:, cols], ga_refs[q][...])

    @pl.when(et == pl.num_programs(1) - 1)
    def _():
        x2 = x1_ref[...] + acc_ref[...].T
        h3 = _rms(x2, ln3_ref[...]).astype(BF16)
        gate = _sigmoid(_dot(h3, wpg_ref[...]))
        x3 = x2 + gate * _dot(pe_ref[...].astype(BF16), wple_ref[...])
        o_ref[...] = _rms(x3, lnf_ref[...]) if final else x3


def _peer(final, h2t, r2, e2, c1, p1, u_bf, vt_bf, x1, pe, ln3, wpg, wple, lnf):
    nt = T_ALL // PEER_TOKENS
    ne = N_EXPERTS // PEER_EXPERTS
    rows_per_step = PEER_EXPERTS // N_KEYS
    tok_spec = lambda width: pl.BlockSpec((PEER_TOKENS, width), lambda t, e: (t, 0))
    lb_per_step = PEER_TOKENS // LANES
    key_spec = pl.BlockSpec((HEADS, lb_per_step, N_KEYS, LANES), lambda t, e: (0, t, 0, 0))
    row_spec = pl.BlockSpec((HEADS, lb_per_step, None, rows_per_step, LANES),
                            lambda t, e: (0, t, e, 0, 0))
    return pl.pallas_call(
        functools.partial(_peer_kernel, final),
        grid=(nt, ne),
        in_specs=[
            pl.BlockSpec((D_MODEL, PEER_TOKENS), lambda t, e: (0, t)),
            key_spec, key_spec, row_spec, row_spec,
            pl.BlockSpec((PEER_EXPERTS, D_MODEL), lambda t, e: (e, 0)),
            pl.BlockSpec((D_MODEL, PEER_EXPERTS), lambda t, e: (0, e)),
            tok_spec(D_MODEL), tok_spec(D_PLE),
            _const_spec((1, D_MODEL), 2),
            _const_spec((D_MODEL, D_MODEL), 2),
            _const_spec((D_PLE, D_MODEL), 2),
            _const_spec((1, D_MODEL), 2),
        ],
        out_specs=tok_spec(D_MODEL),
        out_shape=jax.ShapeDtypeStruct((T_ALL, D_MODEL), F32),
        scratch_shapes=[pltpu.VMEM((D_MODEL, PEER_TOKENS), F32)]
        + [pltpu.VMEM((HEADS, lb_per_step, N_KEYS, LANES), BF16)] * 2
        + [pltpu.VMEM((CHUNK_EXPERTS, PEER_TOKENS), F32)] * PEER_CHUNKS
        + [pltpu.VMEM((CHUNK_EXPERTS, PEER_TOKENS), BF16)] * PEER_CHUNKS,
        compiler_params=pltpu.CompilerParams(
            dimension_semantics=("arbitrary", "arbitrary"), vmem_limit_bytes=VMEM_LIMIT),
        name="peer_dense",
    )(h2t, r2, e2, c1, p1, u_bf, vt_bf, x1, pe, ln3, wpg, wple, lnf)


def kernel(x_prompt, x_sample, state_conv, p_prompt, p_sample, ln1, w_in, ln_v, w_s, b_s, conv_w,
           w_a, w_b, w_o, ln2, w_q, k1, k2, u_tab, v_tab, ln3, w_pg, w_ple, ln_f):
    assert x_prompt.shape == (N_PROMPT_SEQ, PROMPT_LEN, D_MODEL)
    assert x_sample.shape == (N_SAMPLE_SEQ, SAMPLE_LEN, D_MODEL)
    assert u_tab.shape == (DEPTH, N_EXPERTS, D_MODEL)

    x_p = x_prompt.reshape(T_PROMPT, D_MODEL)
    x_s = x_sample.reshape(T_SAMPLE, D_MODEL)
    sample_block = 0
    seqs_per_chunk = CHUNK // SAMPLE_LEN
    eye = jnp.eye(seqs_per_chunk, dtype=F32)
    lnf = ln_f.reshape(1, D_MODEL)
    conv_prompt, conv_sample, v_sample = [], [], []
    y = None

    for i in range(DEPTH):
        row = lambda a: a[i].reshape(1, -1)
        ws_p = w_s[i].astype(BF16)
        corner = w_s[i][:, :SAMPLE_LEN, :SAMPLE_LEN]
        ws_s = (eye[None, :, None, :, None] * corner[:, None, :, None, :]).reshape(
            N_GROUPS, CHUNK, CHUNK).astype(BF16)
        bias_p = jnp.repeat(b_s[i].T, GROUP, axis=1)
        bias_s = jnp.tile(jnp.repeat(b_s[i][:, :SAMPLE_LEN].T, GROUP, axis=1), (seqs_per_chunk, 1))
        shared = (conv_w[i], w_a[i].astype(BF16), w_b[i].astype(BF16), w_o[i].astype(BF16))
        front = (row(ln1), w_in[i].astype(BF16), row(ln_v))
        w_prompt = front + (ws_p, bias_p) + shared
        w_sample = front + (ws_s, bias_s) + shared

        prev = state_conv[i]
        zero = jnp.zeros((N_SAMPLE_SEQ, 1, D_B), F32)
        tap1 = jnp.concatenate([prev[:, 1:2], zero, zero, zero], axis=1).reshape(T_SAMPLE, D_B)
        tap2 = jnp.concatenate([prev[:, 0:1], prev[:, 1:2], zero, zero], axis=1).reshape(T_SAMPLE, D_B)

        x1_p, tail_p = _mixer_prompt(x_p, w_prompt)
        x1_s, v_s, cx_s = _mixer_sample(x_s, sample_block, w_sample, tap1, tap2)
        conv_prompt.append(tail_p)
        conv_sample.append(cx_s.reshape(N_SAMPLE_SEQ, SAMPLE_LEN, D_B)[:, SAMPLE_LEN - 2:])
        v_sample.append(v_s.reshape(N_SAMPLE_SEQ, SAMPLE_LEN, D_A))

        x1 = jnp.concatenate([x1_p, x1_s], axis=0)
        pe = jnp.concatenate([p_prompt[i].reshape(T_PROMPT, D_PLE),
                              p_sample[i].reshape(T_SAMPLE, D_PLE)], axis=0)

        h2t, r2, e2, c1, p1 = _select(x1, row(ln2), w_q[i].T.astype(BF16),
                                      k1[i].astype(BF16), k2[i].astype(BF16))
        final = i == DEPTH - 1
        r2 = r2.reshape(HEADS, T_ALL // LANES, N_KEYS, LANES)
        e2 = e2.reshape(HEADS, T_ALL // LANES, N_KEYS, LANES)
        y = _peer(final, h2t, r2, e2, c1, p1, u_tab[i].astype(BF16), v_tab[i].T.astype(BF16),
                  x1, pe, row(ln3), w_pg[i].astype(BF16), w_ple[i].astype(BF16), lnf)
        x_p = y
        x_s = y
        sample_block = T_PROMPT // T_SAMPLE

    y_prompt = y[:T_PROMPT].reshape(N_PROMPT_SEQ, PROMPT_LEN, D_MODEL)
    y_sample = y[T_PROMPT:].reshape(N_SAMPLE_SEQ, SAMPLE_LEN, D_MODEL)
    return (y_prompt, y_sample, jnp.stack(conv_prompt), jnp.stack(conv_sample), jnp.stack(v_sample))
```

```python
import functools

import jax
import jax.numpy as jnp
from jax import lax
from jax.experimental import pallas as pl
from jax.experimental.pallas import tpu as pltpu

F32 = jnp.float32
BF16 = jnp.bfloat16

D_MODEL = 1024
D_A = 1024
D_B = 1024
D_PLE = 256
CHUNK = 128
N_GROUPS = 8
GROUP = D_A // N_GROUPS
N_KEYS = 128
N_EXPERTS = N_KEYS * N_KEYS
HEADS = 8
D_QUERY = 256
D_HALF = 128
TOPK = 16
EPS = 1e-6
DEPTH = 2

N_PROMPT_SEQ = 8
PROMPT_LEN = 2048
N_SAMPLE_SEQ = 128
SAMPLE_LEN = 4
T_PROMPT = N_PROMPT_SEQ * PROMPT_LEN
T_SAMPLE = N_SAMPLE_SEQ * SAMPLE_LEN
T_ALL = T_PROMPT + T_SAMPLE

LANES = 128
SUBLANES = 8
MIX_ROWS = 256
SEL_TOKENS = 512
PEER_TOKENS = 512
HALF_EXPERTS = 512
PEER_STEPS_PER_BLOCK = N_EXPERTS // (2 * HALF_EXPERTS)
BF16_ROWS = 16
GATE_KEYS = 64
VMEM_LIMIT = 56 * 1024 * 1024
NEG = -1e30


def _rms(x, g):
    ms = jnp.mean(x * x, axis=-1, keepdims=True)
    return x * lax.rsqrt(ms + EPS) * g


def _sigmoid(x):
    return 1.0 / (1.0 + jnp.exp(-x))


def _dot(a, b):
    return jnp.dot(a, b, preferred_element_type=F32)


def _const_spec(shape, grid_rank):
    zeros = (0,) * len(shape)
    if grid_rank == 1:
        return pl.BlockSpec(shape, lambda i: zeros, pipeline_mode=pl.Buffered(1))
    return pl.BlockSpec(shape, lambda i, j: zeros, pipeline_mode=pl.Buffered(1))


def _mixer_front(x, ln1_ref, win_ref, lnv_ref, ws_ref, bias_ref, yin_ref, rows):
    h = _rms(x, ln1_ref[...]).astype(BF16)

    def proj(k):
        return _dot(h, win_ref[:, k * D_MODEL:(k + 1) * D_MODEL])

    u = jax.nn.gelu(proj(0))
    v = _rms(jax.nn.gelu(proj(1)), lnv_ref[...])
    vb = v.astype(BF16)
    r_i = lax.broadcasted_iota(jnp.int32, (CHUNK, CHUNK), 0)
    c_i = lax.broadcasted_iota(jnp.int32, (CHUNK, CHUNK), 1)
    causal = r_i >= c_i
    for g in range(N_GROUPS):
        w = jnp.where(causal, ws_ref[g], jnp.zeros((), BF16))
        cols = slice(g * GROUP, (g + 1) * GROUP)
        for c in range(rows // CHUNK):
            rws = slice(c * CHUNK, (c + 1) * CHUNK)
            s = _dot(w, vb[rws, cols]) + bias_ref[:, cols]
            yin_ref[rws, cols] = (u[rws, cols] * s).astype(BF16)
    return h, v, proj


def _mixer_back(x, proj, yc, wa_ref, wb_ref, wo_ref, yin_ref):
    ya = _dot(yin_ref[...], wa_ref[...])
    yb = _dot((proj(2) * yc).astype(BF16), wb_ref[...])
    mix = _sigmoid(proj(5)) * ya + _sigmoid(proj(6)) * yb
    return x + _dot(mix.astype(BF16), wo_ref[...])


def _mixer_prompt_kernel(x_ref, ln1_ref, win_ref, lnv_ref, ws_ref, bias_ref, cw_ref,
                         wa_ref, wb_ref, wo_ref, x1_ref, tail_ref, yin_ref, cbuf_ref):
    rows = MIX_ROWS
    x = x_ref[...]
    h, v, proj = _mixer_front(x, ln1_ref, win_ref, lnv_ref, ws_ref, bias_ref, yin_ref, rows)
    cx = proj(3) * proj(4)

    @pl.when(pl.program_id(1) == 0)
    def _():
        cbuf_ref[0:SUBLANES, :] = jnp.zeros((SUBLANES, D_B), F32)

    cbuf_ref[SUBLANES:SUBLANES + rows, :] = cx
    c1 = cbuf_ref[SUBLANES - 1:SUBLANES - 1 + rows, :]
    c2 = cbuf_ref[SUBLANES - 2:SUBLANES - 2 + rows, :]
    yc = cw_ref[0:1, :] * c2 + cw_ref[1:2, :] * c1 + cw_ref[2:3, :] * cx
    cbuf_ref[0:SUBLANES, :] = cx[rows - SUBLANES:rows, :]
    tail_ref[0] = cx[rows - 2:rows, :]
    x1_ref[...] = _mixer_back(x, proj, yc, wa_ref, wb_ref, wo_ref, yin_ref)


def _mixer_sample_kernel(x_ref, ln1_ref, win_ref, lnv_ref, ws_ref, bias_ref, cw_ref,
                         wa_ref, wb_ref, wo_ref, p1_ref, p2_ref,
                         x1_ref, v_ref, cx_ref, yin_ref, cbuf_ref):
    rows = T_SAMPLE
    x = x_ref[...]
    h, v, proj = _mixer_front(x, ln1_ref, win_ref, lnv_ref, ws_ref, bias_ref, yin_ref, rows)
    v_ref[...] = v
    cx = proj(3) * proj(4)
    cx_ref[...] = cx
    cbuf_ref[0:SUBLANES, :] = jnp.zeros((SUBLANES, D_B), F32)
    cbuf_ref[SUBLANES:SUBLANES + rows, :] = cx
    pos = lax.broadcasted_iota(jnp.int32, (rows, D_B), 0) & (SAMPLE_LEN - 1)
    c1 = jnp.where(pos >= 1, cbuf_ref[SUBLANES - 1:SUBLANES - 1 + rows, :], p1_ref[...])
    c2 = jnp.where(pos >= 2, cbuf_ref[SUBLANES - 2:SUBLANES - 2 + rows, :], p2_ref[...])
    yc = cw_ref[0:1, :] * c2 + cw_ref[1:2, :] * c1 + cw_ref[2:3, :] * cx
    x1_ref[...] = _mixer_back(x, proj, yc, wa_ref, wb_ref, wo_ref, yin_ref)


def _mixer_weight_specs(grid_rank):
    cs = functools.partial(_const_spec, grid_rank=grid_rank)
    return [
        cs((1, D_MODEL)),
        cs((D_MODEL, 7 * D_MODEL)),
        cs((1, D_A)),
        cs((N_GROUPS, CHUNK, CHUNK)),
        cs((CHUNK, D_A)),
        cs((3, D_B)),
        cs((D_A, D_MODEL)),
        cs((D_B, D_MODEL)),
        cs((D_MODEL, D_MODEL)),
    ]


def _mixer_prompt(x2d, weights):
    nj = PROMPT_LEN // MIX_ROWS
    return pl.pallas_call(
        _mixer_prompt_kernel,
        grid=(N_PROMPT_SEQ, nj),
        in_specs=[pl.BlockSpec((MIX_ROWS, D_MODEL), lambda b, j: (b * nj + j, 0))]
        + _mixer_weight_specs(2),
        out_specs=[
            pl.BlockSpec((MIX_ROWS, D_MODEL), lambda b, j: (b * nj + j, 0)),
            pl.BlockSpec((1, 2, D_B), lambda b, j: (b, 0, 0)),
        ],
        out_shape=[
            jax.ShapeDtypeStruct((T_PROMPT, D_MODEL), F32),
            jax.ShapeDtypeStruct((N_PROMPT_SEQ, 2, D_B), F32),
        ],
        scratch_shapes=[
            pltpu.VMEM((MIX_ROWS, D_A), BF16),
            pltpu.VMEM((MIX_ROWS + SUBLANES, D_B), F32),
        ],
        compiler_params=pltpu.CompilerParams(
            dimension_semantics=("arbitrary", "arbitrary"), vmem_limit_bytes=VMEM_LIMIT),
        name="mixer_prompt",
    )(x2d, *weights)


def _mixer_sample(x2d, row_block, weights, p1, p2):
    full = lambda shape: pl.BlockSpec(shape, lambda i: (0,) * len(shape))
    return pl.pallas_call(
        _mixer_sample_kernel,
        grid=(1,),
        in_specs=[pl.BlockSpec((T_SAMPLE, D_MODEL), lambda i: (row_block, 0))]
        + _mixer_weight_specs(1)
        + [full((T_SAMPLE, D_B)), full((T_SAMPLE, D_B))],
        out_specs=[full((T_SAMPLE, D_MODEL)), full((T_SAMPLE, D_A)), full((T_SAMPLE, D_B))],
        out_shape=[
            jax.ShapeDtypeStruct((T_SAMPLE, D_MODEL), F32),
            jax.ShapeDtypeStruct((T_SAMPLE, D_A), F32),
            jax.ShapeDtypeStruct((T_SAMPLE, D_B), F32),
        ],
        scratch_shapes=[
            pltpu.VMEM((T_SAMPLE, D_A), BF16),
            pltpu.VMEM((T_SAMPLE + SUBLANES, D_B), F32),
        ],
        compiler_params=pltpu.CompilerParams(
            dimension_semantics=("arbitrary",), vmem_limit_bytes=VMEM_LIMIT),
        name="mixer_sample",
    )(x2d, *weights, p1, p2)


def _cmpx(v, i, j):
    hi = jnp.maximum(v[i], v[j])
    lo = jnp.minimum(v[i], v[j])
    v[i] = hi
    v[j] = lo


def _bitonic_merge16(v):
    j = TOPK // 2
    while j >= 1:
        for i in range(TOPK):
            l = i ^ j
            if l > i:
                _cmpx(v, i, l)
        j //= 2


def _sort16(v):
    k = 2
    while k <= TOPK:
        j = k // 2
        while j >= 1:
            for i in range(TOPK):
                l = i ^ j
                if l > i:
                    if (i & k) == 0:
                        _cmpx(v, i, l)
                    else:
                        _cmpx(v, l, i)
            j //= 2
        k *= 2


def _top16(s):
    v = [s[k] for k in range(TOPK)]
    _sort16(v)
    for shift in (4, 2, 1):
        w = [pltpu.roll(v[k], shift, 0) for k in range(TOPK)]
        v = [jnp.maximum(v[k], w[TOPK - 1 - k]) for k in range(TOPK)]
        _bitonic_merge16(v)
    return v


def _sublane_allreduce(x, op):
    for shift in (4, 2, 1):
        x = op(x, pltpu.roll(x, shift, 0))
    return x


def _on_sublanes(vals, sub):
    out = vals[SUBLANES - 1]
    for r in range(SUBLANES - 2, -1, -1):
        out = jnp.where(sub == r, vals[r], out)
    return out


def _select_kernel(x1_ref, ln2_ref, wqt_ref, k1_ref, k2_ref,
                   h2t_ref, r2_ref, e2_ref, c1_ref, p1_ref, s1_scr, s2_scr):
    tokens = SEL_TOKENS
    h2 = _rms(x1_ref[...], ln2_ref[...])
    h2t = h2.T.astype(BF16)
    h2t_ref[...] = pltpu.bitcast(h2t, jnp.uint32)
    qt = _dot(wqt_ref[...], h2t)
    for h in range(HEADS):
        q1 = qt[h * D_QUERY:h * D_QUERY + D_HALF].astype(BF16)
        q2 = qt[h * D_QUERY + D_HALF:(h + 1) * D_QUERY].astype(BF16)
        s1_scr[h] = _dot(k1_ref[...], q1).reshape(TOPK, SUBLANES, tokens)
        s2_scr[h] = _dot(k2_ref[...], q2).reshape(TOPK, SUBLANES, tokens)

    sub = lax.broadcasted_iota(jnp.int32, (SUBLANES, LANES), 0)

    def lane_block(lb, carry):
        lanes = pl.ds(pl.multiple_of(lb * LANES, LANES), LANES)
        for h in range(HEADS):
            s1 = s1_scr[h, :, :, lanes]
            s2 = s2_scr[h, :, :, lanes]
            t1 = _top16(s1)
            t2 = _top16(s2)
            t2lo = _on_sublanes(t2[0:8], sub)
            t2hi = _on_sublanes(t2[8:16], sub)
            t1hi = _on_sublanes(t1[8:16], sub)
            cands = [t1[0] + t2lo, t1[0] + t2hi, t1[1] + t2lo]
            for a, nb in ((2, 5), (3, 4), (4, 3), (5, 2), (6, 2), (7, 2)):
                cands.append(jnp.where(sub < nb, t1[a] + t2lo, NEG))
            cands.append(t1hi + t2[0])
            cur = list(cands)
            tops = []
            for it in range(TOPK + 1):
                m = functools.reduce(jnp.maximum, cur)
                m = _sublane_allreduce(m, jnp.maximum)
                tops.append(m)
                if it < TOPK:
                    cur = [jnp.where(c == m, NEG, c) for c in cur]
            tau = 0.5 * (tops[TOPK - 1] + tops[TOPK])
            z = functools.reduce(
                jnp.add, [jnp.where(c >= tau, jnp.exp(c - tops[0]), 0.0) for c in cands])
            z = _sublane_allreduce(z, jnp.add)
            inv_z = 1.0 / z
            p1 = jnp.where(s1 >= t1[TOPK - 1][None], jnp.exp(s1 - t1[0][None]) * inv_z[None], 0.0)
            e2 = jnp.where(s2 >= t2[TOPK - 1][None], jnp.exp(s2 - t2[0][None]), 0.0)
            th = tau[None] - s1
            rank2 = jnp.zeros_like(s2)
            count1 = jnp.zeros_like(s1)
            for b in range(TOPK):
                rank2 = rank2 + jnp.where(t2[b][None] > s2, 1.0, 0.0)
                count1 = count1 + jnp.where(t2[b][None] >= th, 1.0, 0.0)
            r2_ref[h, lb] = rank2
            e2_ref[h, lb] = e2
            c1_ref[h, lb] = count1
            p1_ref[h, lb] = p1
        return carry

    lax.fori_loop(0, tokens // LANES, lane_block, 0)


def _select(x1, ln2, wqt, k1, k2):
    nb = T_ALL // SEL_TOKENS
    lb_per_step = SEL_TOKENS // LANES
    n_lb = T_ALL // LANES
    key_shape = jax.ShapeDtypeStruct((HEADS, n_lb, TOPK, SUBLANES, LANES), F32)
    key_spec = pl.BlockSpec((HEADS, lb_per_step, TOPK, SUBLANES, LANES), lambda i: (0, i, 0, 0, 0))
    scr = pltpu.VMEM((HEADS, TOPK, SUBLANES, SEL_TOKENS), F32)
    return pl.pallas_call(
        _select_kernel,
        grid=(nb,),
        in_specs=[
            pl.BlockSpec((SEL_TOKENS, D_MODEL), lambda i: (i, 0)),
            _const_spec((1, D_MODEL), 1),
            _const_spec((HEADS * D_QUERY, D_MODEL), 1),
            _const_spec((N_KEYS, D_HALF), 1),
            _const_spec((N_KEYS, D_HALF), 1),
        ],
        out_specs=[pl.BlockSpec((D_MODEL // 2, SEL_TOKENS), lambda i: (0, i))] + [key_spec] * 4,
        out_shape=[jax.ShapeDtypeStruct((D_MODEL // 2, T_ALL), jnp.uint32)] + [key_shape] * 4,
        scratch_shapes=[scr, scr],
        compiler_params=pltpu.CompilerParams(
            dimension_semantics=("arbitrary",), vmem_limit_bytes=VMEM_LIMIT),
        name="peer_select",
    )(x1, ln2, wqt, k1, k2)


def _peer_kernel(final, h2t_ref, r2_ref, e2_ref, c1a_ref, p1a_ref, c1b_ref, p1b_ref, u_ref, vt_ref,
                 x1_ref, pe_ref, ln3_ref, wpg_ref, wple_ref, lnf_ref, o_ref,
                 acc_ref, r2_scr, e2_scr, at0_ref, at1_ref, ga0_ref, ga1_ref):
    j = pl.program_id(0)
    phase = j % PEER_STEPS_PER_BLOCK
    rows_half = HALF_EXPERTS // N_KEYS
    vregs = GATE_KEYS // BF16_ROWS

    def load_keys():
        for h in range(HEADS):
            r2_scr[h] = r2_ref[h].astype(BF16)
            e2_scr[h] = e2_ref[h].astype(BF16)

    @pl.when(j == 0)
    def _():
        at1_ref[...] = jnp.zeros_like(at1_ref)
        ga0_ref[...] = jnp.zeros_like(ga0_ref)
        ga1_ref[...] = jnp.zeros_like(ga1_ref)
        load_keys()

    @pl.when((phase == 1) | (j == 0))
    def _():
        acc_ref[...] = jnp.zeros_like(acc_ref)

    def lane_bcast(ref, h, lb, r):
        return jnp.broadcast_to(ref[h, lb, r:r + 1, :], (BF16_ROWS, LANES)).astype(BF16)

    def gate_tile(at_ref, ga_ref, c1_ref, p1_ref, row0, lb, kq):
        lanes = slice(lb * LANES, (lb + 1) * LANES)
        keys = [slice(kq * GATE_KEYS + k * BF16_ROWS, kq * GATE_KEYS + (k + 1) * BF16_ROWS)
                for k in range(vregs)]
        g = [[None] * vregs for _ in range(rows_half)]
        for h in range(HEADS):
            r2 = [r2_scr[h, lb, ks, :] for ks in keys]
            e2 = [e2_scr[h, lb, ks, :] for ks in keys]
            for r in range(rows_half):
                c = lane_bcast(c1_ref, h, lb, row0 + r)
                p = lane_bcast(p1_ref, h, lb, row0 + r)
                for k in range(vregs):
                    term = jnp.where(r2[k] < c, e2[k], jnp.zeros_like(e2[k])) * p
                    g[r][k] = term if h == 0 else g[r][k] + term
        for r in range(rows_half):
            for k in range(vregs):
                rows = slice(r * N_KEYS + keys[k].start, r * N_KEYS + keys[k].stop)
                a = at_ref[rows, lanes].astype(BF16)
                ga_ref[rows, lanes] = g[r][k] * jax.nn.gelu(a)

    def half_step(half, at_new, at_cur, ga_new, ga_old, c1_ref, p1_ref, row0):
        u_rows = slice(half * HALF_EXPERTS // 2, (half + 1) * HALF_EXPERTS // 2)
        v_cols = slice(half * HALF_EXPERTS, (half + 1) * HALF_EXPERTS)
        at_new[...] = _dot(pltpu.bitcast(u_ref[u_rows, :], BF16), pltpu.bitcast(h2t_ref[...], BF16))
        for lb in range(PEER_TOKENS // LANES):
            for kq in range(N_KEYS // GATE_KEYS):
                gate_tile(at_cur, ga_new, c1_ref, p1_ref, row0, lb, kq)
        acc_ref[...] += _dot(pltpu.bitcast(vt_ref[:, v_cols], BF16), ga_old[...])

    half_step(0, at0_ref, at1_ref, ga1_ref, ga0_ref, c1a_ref, p1a_ref, rows_half)

    @pl.when(phase == 0)
    def _():
        load_keys()

    half_step(1, at1_ref, at0_ref, ga0_ref, ga1_ref, c1b_ref, p1b_ref, 0)

    @pl.when((phase == 0) & (j > 0))
    def _():
        x2 = x1_ref[...] + acc_ref[...].T
        h3 = _rms(x2, ln3_ref[...]).astype(BF16)
        gate = _sigmoid(_dot(h3, wpg_ref[...]))
        x3 = x2 + gate * _dot(pe_ref[...].astype(BF16), wple_ref[...])
        o_ref[...] = _rms(x3, lnf_ref[...]) if final else x3


def _peer(final, h2t, r2, e2, c1, p1, u_pk, vt_pk, x1, pe, ln3, wpg, wple, lnf):
    nt = T_ALL // PEER_TOKENS
    tiles_per_block = N_EXPERTS // HALF_EXPERTS
    n_tiles = nt * tiles_per_block
    assert tiles_per_block == 2 * PEER_STEPS_PER_BLOCK
    n_steps = n_tiles // 2 + 1
    lb_per_step = PEER_TOKENS // LANES

    prev_step = lambda j: jnp.maximum(j - 1, 0)
    tok_spec = lambda width: pl.BlockSpec(
        (PEER_TOKENS, width), lambda j: (prev_step(j) // PEER_STEPS_PER_BLOCK, 0))
    cur_block = lambda j: jnp.minimum(j // PEER_STEPS_PER_BLOCK, nt - 1)
    key_spec = pl.BlockSpec((HEADS, lb_per_step, N_KEYS, LANES), lambda j: (0, cur_block(j), 0, 0))

    def row_spec(tile_of_step):
        def index(j):
            s = jnp.clip(tile_of_step(j), 0, n_tiles - 1)
            return (0, s // tiles_per_block, (s % tiles_per_block) // 2, 0, 0)
        return pl.BlockSpec((HEADS, lb_per_step, None, SUBLANES, LANES), index)

    rows_a = row_spec(lambda j: 2 * j - 1)
    rows_b = row_spec(lambda j: 2 * j)
    const = lambda shape: pl.BlockSpec(shape, lambda j: (0,) * len(shape), pipeline_mode=pl.Buffered(1))
    return pl.pallas_call(
        functools.partial(_peer_kernel, final),
        grid=(n_steps,),
        in_specs=[
            pl.BlockSpec((D_MODEL // 2, PEER_TOKENS), lambda j: (0, cur_block(j))),
            key_spec, key_spec, rows_a, rows_a, rows_b, rows_b,
            pl.BlockSpec((HALF_EXPERTS, D_MODEL), lambda j: (j % PEER_STEPS_PER_BLOCK, 0)),
            pl.BlockSpec((D_MODEL // 2, 2 * HALF_EXPERTS),
                         lambda j: (0, prev_step(j) % PEER_STEPS_PER_BLOCK)),
            tok_spec(D_MODEL), tok_spec(D_PLE),
            const((1, D_MODEL)), const((D_MODEL, D_MODEL)), const((D_PLE, D_MODEL)), const((1, D_MODEL)),
        ],
        out_specs=tok_spec(D_MODEL),
        out_shape=jax.ShapeDtypeStruct((T_ALL, D_MODEL), F32),
        scratch_shapes=[pltpu.VMEM((D_MODEL, PEER_TOKENS), F32)]
        + [pltpu.VMEM((HEADS, lb_per_step, N_KEYS, LANES), BF16)] * 2
        + [pltpu.VMEM((HALF_EXPERTS, PEER_TOKENS), F32)] * 2
        + [pltpu.VMEM((HALF_EXPERTS, PEER_TOKENS), BF16)] * 2,
        compiler_params=pltpu.CompilerParams(
            dimension_semantics=("arbitrary",), vmem_limit_bytes=VMEM_LIMIT),
        name="peer_dense",
    )(h2t, r2, e2, c1, p1, c1, p1, u_pk, vt_pk, x1, pe, ln3, wpg, wple, lnf)


def _pack_row_pairs(x):
    r, c = x.shape
    bits = lax.bitcast_convert_type(x.astype(BF16), jnp.uint16).astype(jnp.uint32).reshape(r // 2, 2, c)
    return bits[:, 0, :] | (bits[:, 1, :] << 16)


def kernel(x_prompt, x_sample, state_conv, p_prompt, p_sample, ln1, w_in, ln_v, w_s, b_s, conv_w,
           w_a, w_b, w_o, ln2, w_q, k1, k2, u_tab, v_tab, ln3, w_pg, w_ple, ln_f):
    assert x_prompt.shape == (N_PROMPT_SEQ, PROMPT_LEN, D_MODEL)
    assert x_sample.shape == (N_SAMPLE_SEQ, SAMPLE_LEN, D_MODEL)
    assert u_tab.shape == (DEPTH, N_EXPERTS, D_MODEL)

    x_p = x_prompt.reshape(T_PROMPT, D_MODEL)
    x_s = x_sample.reshape(T_SAMPLE, D_MODEL)
    sample_block = 0
    seqs_per_chunk = CHUNK // SAMPLE_LEN
    eye = jnp.eye(seqs_per_chunk, dtype=F32)
    lnf = ln_f.reshape(1, D_MODEL)
    conv_prompt, conv_sample, v_sample = [], [], []
    y = None

    for i in range(DEPTH):
        row = lambda a: a[i].reshape(1, -1)
        ws_p = w_s[i].astype(BF16)
        corner = w_s[i][:, :SAMPLE_LEN, :SAMPLE_LEN]
        ws_s = (eye[None, :, None, :, None] * corner[:, None, :, None, :]).reshape(
            N_GROUPS, CHUNK, CHUNK).astype(BF16)
        bias_p = jnp.repeat(b_s[i].T, GROUP, axis=1)
        bias_s = jnp.tile(jnp.repeat(b_s[i][:, :SAMPLE_LEN].T, GROUP, axis=1), (seqs_per_chunk, 1))
        shared = (conv_w[i], w_a[i].astype(BF16), w_b[i].astype(BF16), w_o[i].astype(BF16))
        front = (row(ln1), w_in[i].astype(BF16), row(ln_v))
        w_prompt = front + (ws_p, bias_p) + shared
        w_sample = front + (ws_s, bias_s) + shared

        prev = state_conv[i]
        zero = jnp.zeros((N_SAMPLE_SEQ, 1, D_B), F32)
        tap1 = jnp.concatenate([prev[:, 1:2], zero, zero, zero], axis=1).reshape(T_SAMPLE, D_B)
        tap2 = jnp.concatenate([prev[:, 0:1], prev[:, 1:2], zero, zero], axis=1).reshape(T_SAMPLE, D_B)

        x1_p, tail_p = _mixer_prompt(x_p, w_prompt)
        x1_s, v_s, cx_s = _mixer_sample(x_s, sample_block, w_sample, tap1, tap2)
        conv_prompt.append(tail_p)
        conv_sample.append(cx_s.reshape(N_SAMPLE_SEQ, SAMPLE_LEN, D_B)[:, SAMPLE_LEN - 2:])
        v_sample.append(v_s.reshape(N_SAMPLE_SEQ, SAMPLE_LEN, D_A))

        x1 = jnp.concatenate([x1_p, x1_s], axis=0)
        pe = jnp.concatenate([p_prompt[i].reshape(T_PROMPT, D_PLE),
                              p_sample[i].reshape(T_SAMPLE, D_PLE)], axis=0)

        h2t, r2, e2, c1, p1 = _select(x1, row(ln2), w_q[i].T.astype(BF16),
                                      k1[i].astype(BF16), k2[i].astype(BF16))
        final = i == DEPTH - 1
        r2 = r2.reshape(HEADS, T_ALL // LANES, N_KEYS, LANES)
        e2 = e2.reshape(HEADS, T_ALL // LANES, N_KEYS, LANES)
        y = _peer(final, h2t, r2, e2, c1, p1, _pack_row_pairs(u_tab[i]), _pack_row_pairs(v_tab[i].T),
                  x1, pe, row(ln3), w_pg[i].astype(BF16), w_ple[i].astype(BF16), lnf)
        x_p = y
        x_s = y
        sample_block = T_PROMPT // T_SAMPLE

    y_prompt = y[:T_PROMPT].reshape(N_PROMPT_SEQ, PROMPT_LEN, D_MODEL)
    y_sample = y[T_PROMPT:].reshape(N_SAMPLE_SEQ, SAMPLE_LEN, D_MODEL)
    return (y_prompt, y_sample, jnp.stack(conv_prompt), jnp.stack(conv_sample), jnp.stack(v_sample))
```

```python
import functools

import jax
import jax.numpy as jnp
from jax import lax
from jax.experimental import pallas as pl
from jax.experimental.pallas import tpu as pltpu

F32 = jnp.float32
BF16 = jnp.bfloat16

D_MODEL = 1024
D_A = 1024
D_B = 1024
D_PLE = 256
CHUNK = 128
N_GROUPS = 8
GROUP = D_A // N_GROUPS
N_KEYS = 128
N_EXPERTS = N_KEYS * N_KEYS
HEADS = 8
D_QUERY = 256
D_HALF = 128
TOPK = 16
EPS = 1e-6
DEPTH = 2

N_PROMPT_SEQ = 8
PROMPT_LEN = 2048
N_SAMPLE_SEQ = 128
SAMPLE_LEN = 4
T_PROMPT = N_PROMPT_SEQ * PROMPT_LEN
T_SAMPLE = N_SAMPLE_SEQ * SAMPLE_LEN
T_ALL = T_PROMPT + T_SAMPLE

LANES = 128
SUBLANES = 8
MIX_ROWS = 256
SEL_TOKENS = 512
PEER_TOKENS = 512
HALF_EXPERTS = 512
PEER_STEPS_PER_BLOCK = N_EXPERTS // (2 * HALF_EXPERTS)
GATE_ROWS = 4
PACK_EXPERTS = 1024
BF16_ROWS = 16
GATE_KEYS = 64
VMEM_LIMIT = 56 * 1024 * 1024
NEG = -1e30


def _rms(x, g):
    ms = jnp.mean(x * x, axis=-1, keepdims=True)
    return x * lax.rsqrt(ms + EPS) * g


def _sigmoid(x):
    return 1.0 / (1.0 + jnp.exp(-x))


def _dot(a, b):
    return jnp.dot(a, b, preferred_element_type=F32)


def _const_spec(shape, grid_rank):
    zeros = (0,) * len(shape)
    if grid_rank == 1:
        return pl.BlockSpec(shape, lambda i: zeros, pipeline_mode=pl.Buffered(1))
    return pl.BlockSpec(shape, lambda i, j: zeros, pipeline_mode=pl.Buffered(1))


def _mixer_front(x, ln1_ref, win_ref, lnv_ref, ws_ref, bias_ref, yin_ref, rows):
    h = _rms(x, ln1_ref[...]).astype(BF16)

    def proj(k):
        return _dot(h, win_ref[:, k * D_MODEL:(k + 1) * D_MODEL])

    u = jax.nn.gelu(proj(0))
    v = _rms(jax.nn.gelu(proj(1)), lnv_ref[...])
    vb = v.astype(BF16)
    r_i = lax.broadcasted_iota(jnp.int32, (CHUNK, CHUNK), 0)
    c_i = lax.broadcasted_iota(jnp.int32, (CHUNK, CHUNK), 1)
    causal = r_i >= c_i
    for g in range(N_GROUPS):
        w = jnp.where(causal, ws_ref[g], jnp.zeros((), BF16))
        cols = slice(g * GROUP, (g + 1) * GROUP)
        for c in range(rows // CHUNK):
            rws = slice(c * CHUNK, (c + 1) * CHUNK)
            s = _dot(w, vb[rws, cols]) + bias_ref[:, cols]
            yin_ref[rws, cols] = (u[rws, cols] * s).astype(BF16)
    return h, v, proj


def _mixer_back(x, proj, yc, wa_ref, wb_ref, wo_ref, yin_ref):
    ya = _dot(yin_ref[...], wa_ref[...])
    yb = _dot((proj(2) * yc).astype(BF16), wb_ref[...])
    mix = _sigmoid(proj(5)) * ya + _sigmoid(proj(6)) * yb
    return x + _dot(mix.astype(BF16), wo_ref[...])


def _mixer_prompt_kernel(x_ref, ln1_ref, win_ref, lnv_ref, ws_ref, bias_ref, cw_ref,
                         wa_ref, wb_ref, wo_ref, x1_ref, tail_ref, yin_ref, cbuf_ref):
    rows = MIX_ROWS
    x = x_ref[...]
    h, v, proj = _mixer_front(x, ln1_ref, win_ref, lnv_ref, ws_ref, bias_ref, yin_ref, rows)
    cx = proj(3) * proj(4)

    @pl.when(pl.program_id(1) == 0)
    def _():
        cbuf_ref[0:SUBLANES, :] = jnp.zeros((SUBLANES, D_B), F32)

    cbuf_ref[SUBLANES:SUBLANES + rows, :] = cx
    c1 = cbuf_ref[SUBLANES - 1:SUBLANES - 1 + rows, :]
    c2 = cbuf_ref[SUBLANES - 2:SUBLANES - 2 + rows, :]
    yc = cw_ref[0:1, :] * c2 + cw_ref[1:2, :] * c1 + cw_ref[2:3, :] * cx
    cbuf_ref[0:SUBLANES, :] = cx[rows - SUBLANES:rows, :]
    tail_ref[0] = cx[rows - 2:rows, :]
    x1_ref[...] = _mixer_back(x, proj, yc, wa_ref, wb_ref, wo_ref, yin_ref)


def _mixer_sample_kernel(x_ref, ln1_ref, win_ref, lnv_ref, ws_ref, bias_ref, cw_ref,
                         wa_ref, wb_ref, wo_ref, p1_ref, p2_ref,
                         x1_ref, v_ref, cx_ref, yin_ref, cbuf_ref):
    rows = T_SAMPLE
    x = x_ref[...]
    h, v, proj = _mixer_front(x, ln1_ref, win_ref, lnv_ref, ws_ref, bias_ref, yin_ref, rows)
    v_ref[...] = v
    cx = proj(3) * proj(4)
    cx_ref[...] = cx
    cbuf_ref[0:SUBLANES, :] = jnp.zeros((SUBLANES, D_B), F32)
    cbuf_ref[SUBLANES:SUBLANES + rows, :] = cx
    pos = lax.broadcasted_iota(jnp.int32, (rows, D_B), 0) & (SAMPLE_LEN - 1)
    c1 = jnp.where(pos >= 1, cbuf_ref[SUBLANES - 1:SUBLANES - 1 + rows, :], p1_ref[...])
    c2 = jnp.where(pos >= 2, cbuf_ref[SUBLANES - 2:SUBLANES - 2 + rows, :], p2_ref[...])
    yc = cw_ref[0:1, :] * c2 + cw_ref[1:2, :] * c1 + cw_ref[2:3, :] * cx
    x1_ref[...] = _mixer_back(x, proj, yc, wa_ref, wb_ref, wo_ref, yin_ref)


def _mixer_weight_specs(grid_rank):
    cs = functools.partial(_const_spec, grid_rank=grid_rank)
    return [
        cs((1, D_MODEL)),
        cs((D_MODEL, 7 * D_MODEL)),
        cs((1, D_A)),
        cs((N_GROUPS, CHUNK, CHUNK)),
        cs((CHUNK, D_A)),
        cs((3, D_B)),
        cs((D_A, D_MODEL)),
        cs((D_B, D_MODEL)),
        cs((D_MODEL, D_MODEL)),
    ]


def _mixer_prompt(x2d, weights):
    nj = PROMPT_LEN // MIX_ROWS
    return pl.pallas_call(
        _mixer_prompt_kernel,
        grid=(N_PROMPT_SEQ, nj),
        in_specs=[pl.BlockSpec((MIX_ROWS, D_MODEL), lambda b, j: (b * nj + j, 0))]
        + _mixer_weight_specs(2),
        out_specs=[
            pl.BlockSpec((MIX_ROWS, D_MODEL), lambda b, j: (b * nj + j, 0)),
            pl.BlockSpec((1, 2, D_B), lambda b, j: (b, 0, 0)),
        ],
        out_shape=[
            jax.ShapeDtypeStruct((T_PROMPT, D_MODEL), F32),
            jax.ShapeDtypeStruct((N_PROMPT_SEQ, 2, D_B), F32),
        ],
        scratch_shapes=[
            pltpu.VMEM((MIX_ROWS, D_A), BF16),
            pltpu.VMEM((MIX_ROWS + SUBLANES, D_B), F32),
        ],
        compiler_params=pltpu.CompilerParams(
            dimension_semantics=("arbitrary", "arbitrary"), vmem_limit_bytes=VMEM_LIMIT),
        name="mixer_prompt",
    )(x2d, *weights)


def _mixer_sample(x2d, row_block, weights, p1, p2):
    full = lambda shape: pl.BlockSpec(shape, lambda i: (0,) * len(shape))
    return pl.pallas_call(
        _mixer_sample_kernel,
        grid=(1,),
        in_specs=[pl.BlockSpec((T_SAMPLE, D_MODEL), lambda i: (row_block, 0))]
        + _mixer_weight_specs(1)
        + [full((T_SAMPLE, D_B)), full((T_SAMPLE, D_B))],
        out_specs=[full((T_SAMPLE, D_MODEL)), full((T_SAMPLE, D_A)), full((T_SAMPLE, D_B))],
        out_shape=[
            jax.ShapeDtypeStruct((T_SAMPLE, D_MODEL), F32),
            jax.ShapeDtypeStruct((T_SAMPLE, D_A), F32),
            jax.ShapeDtypeStruct((T_SAMPLE, D_B), F32),
        ],
        scratch_shapes=[
            pltpu.VMEM((T_SAMPLE, D_A), BF16),
            pltpu.VMEM((T_SAMPLE + SUBLANES, D_B), F32),
        ],
        compiler_params=pltpu.CompilerParams(
            dimension_semantics=("arbitrary",), vmem_limit_bytes=VMEM_LIMIT),
        name="mixer_sample",
    )(x2d, *weights, p1, p2)


def _cmpx(v, i, j):
    hi = jnp.maximum(v[i], v[j])
    lo = jnp.minimum(v[i], v[j])
    v[i] = hi
    v[j] = lo


def _bitonic_merge16(v):
    j = TOPK // 2
    while j >= 1:
        for i in range(TOPK):
            l = i ^ j
            if l > i:
                _cmpx(v, i, l)
        j //= 2


def _sort16(v):
    k = 2
    while k <= TOPK:
        j = k // 2
        while j >= 1:
            for i in range(TOPK):
                l = i ^ j
                if l > i:
                    if (i & k) == 0:
                        _cmpx(v, i, l)
                    else:
                        _cmpx(v, l, i)
            j //= 2
        k *= 2


def _top16(s):
    v = [s[k] for k in range(TOPK)]
    _sort16(v)
    for shift in (4, 2, 1):
        w = [pltpu.roll(v[k], shift, 0) for k in range(TOPK)]
        v = [jnp.maximum(v[k], w[TOPK - 1 - k]) for k in range(TOPK)]
        _bitonic_merge16(v)
    return v


def _sublane_allreduce(x, op):
    for shift in (4, 2, 1):
        x = op(x, pltpu.roll(x, shift, 0))
    return x


def _on_sublanes(vals, sub):
    out = vals[SUBLANES - 1]
    for r in range(SUBLANES - 2, -1, -1):
        out = jnp.where(sub == r, vals[r], out)
    return out


def _select_kernel(x1_ref, ln2_ref, wqt_ref, k1_ref, k2_ref,
                   h2t_ref, r2_ref, e2_ref, c1_ref, p1_ref, s1_scr, s2_scr):
    tokens = SEL_TOKENS
    h2 = _rms(x1_ref[...], ln2_ref[...])
    h2t = h2.T.astype(BF16)
    h2t_ref[...] = pltpu.bitcast(h2t, jnp.uint32)
    qt = _dot(wqt_ref[...], h2t)
    for h in range(HEADS):
        q1 = qt[h * D_QUERY:h * D_QUERY + D_HALF].astype(BF16)
        q2 = qt[h * D_QUERY + D_HALF:(h + 1) * D_QUERY].astype(BF16)
        s1_scr[h] = _dot(k1_ref[...], q1).reshape(TOPK, SUBLANES, tokens)
        s2_scr[h] = _dot(k2_ref[...], q2).reshape(TOPK, SUBLANES, tokens)

    sub = lax.broadcasted_iota(jnp.int32, (SUBLANES, LANES), 0)

    def lane_block(lb, carry):
        lanes = pl.ds(pl.multiple_of(lb * LANES, LANES), LANES)
        for h in range(HEADS):
            s1 = s1_scr[h, :, :, lanes]
            s2 = s2_scr[h, :, :, lanes]
            t1 = _top16(s1)
            t2 = _top16(s2)
            t2lo = _on_sublanes(t2[0:8], sub)
            t2hi = _on_sublanes(t2[8:16], sub)
            t1hi = _on_sublanes(t1[8:16], sub)
            cands = [t1[0] + t2lo, t1[0] + t2hi, t1[1] + t2lo]
            for a, nb in ((2, 5), (3, 4), (4, 3), (5, 2), (6, 2), (7, 2)):
                cands.append(jnp.where(sub < nb, t1[a] + t2lo, NEG))
            cands.append(t1hi + t2[0])
            cur = list(cands)
            tops = []
            for it in range(TOPK + 1):
                m = functools.reduce(jnp.maximum, cur)
                m = _sublane_allreduce(m, jnp.maximum)
                tops.append(m)
                if it < TOPK:
                    cur = [jnp.where(c == m, NEG, c) for c in cur]
            tau = 0.5 * (tops[TOPK - 1] + tops[TOPK])
            z = functools.reduce(
                jnp.add, [jnp.where(c >= tau, jnp.exp(c - tops[0]), 0.0) for c in cands])
            z = _sublane_allreduce(z, jnp.add)
            inv_z = 1.0 / z
            p1 = jnp.where(s1 >= t1[TOPK - 1][None], jnp.exp(s1 - t1[0][None]) * inv_z[None], 0.0)
            e2 = jnp.where(s2 >= t2[TOPK - 1][None], jnp.exp(s2 - t2[0][None]), 0.0)
            th = tau[None] - s1
            rank2 = jnp.zeros_like(s2)
            count1 = jnp.zeros_like(s1)
            for b in range(TOPK):
                rank2 = rank2 + jnp.where(t2[b][None] > s2, 1.0, 0.0)
                count1 = count1 + jnp.where(t2[b][None] >= th, 1.0, 0.0)
            r2_ref[h, lb] = rank2
            e2_ref[h, lb] = e2
            c1_ref[h, lb] = count1
            p1_ref[h, lb] = p1
        return carry

    lax.fori_loop(0, tokens // LANES, lane_block, 0)


def _select(x1, ln2, wqt, k1, k2):
    nb = T_ALL // SEL_TOKENS
    lb_per_step = SEL_TOKENS // LANES
    n_lb = T_ALL // LANES
    key_shape = jax.ShapeDtypeStruct((HEADS, n_lb, TOPK, SUBLANES, LANES), F32)
    key_spec = pl.BlockSpec((HEADS, lb_per_step, TOPK, SUBLANES, LANES), lambda i: (0, i, 0, 0, 0))
    scr = pltpu.VMEM((HEADS, TOPK, SUBLANES, SEL_TOKENS), F32)
    return pl.pallas_call(
        _select_kernel,
        grid=(nb,),
        in_specs=[
            pl.BlockSpec((SEL_TOKENS, D_MODEL), lambda i: (i, 0)),
            _const_spec((1, D_MODEL), 1),
            _const_spec((HEADS * D_QUERY, D_MODEL), 1),
            _const_spec((N_KEYS, D_HALF), 1),
            _const_spec((N_KEYS, D_HALF), 1),
        ],
        out_specs=[pl.BlockSpec((D_MODEL // 2, SEL_TOKENS), lambda i: (0, i))] + [key_spec] * 4,
        out_shape=[jax.ShapeDtypeStruct((D_MODEL // 2, T_ALL), jnp.uint32)] + [key_shape] * 4,
        scratch_shapes=[scr, scr],
        compiler_params=pltpu.CompilerParams(
            dimension_semantics=("arbitrary",), vmem_limit_bytes=VMEM_LIMIT),
        name="peer_select",
    )(x1, ln2, wqt, k1, k2)


def _peer_kernel(final, h2t_ref, r2_ref, e2_ref, c1a_ref, p1a_ref, c1b_ref, p1b_ref, u_ref, vt_ref,
                 x1_ref, pe_ref, ln3_ref, wpg_ref, wple_ref, lnf_ref, o_ref,
                 acc_ref, r2_scr, e2_scr, at0_ref, at1_ref, ga0_ref, ga1_ref):
    j = pl.program_id(0)
    phase = j % PEER_STEPS_PER_BLOCK
    rows_half = HALF_EXPERTS // N_KEYS
    vregs = GATE_KEYS // BF16_ROWS

    def load_keys():
        for h in range(HEADS):
            r2_scr[h] = r2_ref[h].astype(BF16)
            e2_scr[h] = e2_ref[h].astype(BF16)

    @pl.when(j == 0)
    def _():
        at1_ref[...] = jnp.zeros_like(at1_ref)
        ga0_ref[...] = jnp.zeros_like(ga0_ref)
        ga1_ref[...] = jnp.zeros_like(ga1_ref)
        load_keys()

    @pl.when((phase == 1) | (j == 0))
    def _():
        acc_ref[...] = jnp.zeros_like(acc_ref)

    def lane_bcast(ref, h, lb, r):
        return jnp.broadcast_to(ref[h, lb, r:r + 1, :], (BF16_ROWS, LANES)).astype(BF16)

    def gate_tile(at_ref, ga_ref, c1_ref, p1_ref, row0, lb, rg, kq):
        lanes = slice(lb * LANES, (lb + 1) * LANES)
        keys = [slice(kq * GATE_KEYS + k * BF16_ROWS, kq * GATE_KEYS + (k + 1) * BF16_ROWS)
                for k in range(vregs)]
        key_rows = range(rg * GATE_ROWS, (rg + 1) * GATE_ROWS)
        g = {r: [None] * vregs for r in key_rows}
        for h in range(HEADS):
            r2 = [r2_scr[h, lb, ks, :] for ks in keys]
            e2 = [e2_scr[h, lb, ks, :] for ks in keys]
            for r in key_rows:
                c = lane_bcast(c1_ref, h, lb, row0 + r)
                p = lane_bcast(p1_ref, h, lb, row0 + r)
                for k in range(vregs):
                    term = jnp.where(r2[k] < c, e2[k], jnp.zeros_like(e2[k])) * p
                    g[r][k] = term if h == 0 else g[r][k] + term
        for r in key_rows:
            for k in range(vregs):
                rows = slice(r * N_KEYS + keys[k].start, r * N_KEYS + keys[k].stop)
                a = at_ref[rows, lanes].astype(BF16)
                ga_ref[rows, lanes] = g[r][k] * jax.nn.gelu(a)

    def half_step(half, at_new, at_cur, ga_new, ga_old, c1_ref, p1_ref, row0):
        u_rows = slice(half * HALF_EXPERTS // 2, (half + 1) * HALF_EXPERTS // 2)
        v_cols = slice(half * HALF_EXPERTS, (half + 1) * HALF_EXPERTS)
        at_new[...] = _dot(pltpu.bitcast(u_ref[u_rows, :], BF16), pltpu.bitcast(h2t_ref[...], BF16))
        for lb in range(PEER_TOKENS // LANES):
            for rg in range(rows_half // GATE_ROWS):
                for kq in range(N_KEYS // GATE_KEYS):
                    gate_tile(at_cur, ga_new, c1_ref, p1_ref, row0, lb, rg, kq)
        acc_ref[...] += _dot(pltpu.bitcast(vt_ref[:, v_cols], BF16), ga_old[...])

    half_step(0, at0_ref, at1_ref, ga1_ref, ga0_ref, c1a_ref, p1a_ref, rows_half)

    @pl.when(phase == 0)
    def _():
        load_keys()

    half_step(1, at1_ref, at0_ref, ga0_ref, ga1_ref, c1b_ref, p1b_ref, 0)

    @pl.when((phase == 0) & (j > 0))
    def _():
        x2 = x1_ref[...] + acc_ref[...].T
        h3 = _rms(x2, ln3_ref[...]).astype(BF16)
        gate = _sigmoid(_dot(h3, wpg_ref[...]))
        x3 = x2 + gate * _dot(pe_ref[...].astype(BF16), wple_ref[...])
        o_ref[...] = _rms(x3, lnf_ref[...]) if final else x3


def _peer(final, h2t, r2, e2, c1, p1, u_pk, vt_pk, x1, pe, ln3, wpg, wple, lnf):
    nt = T_ALL // PEER_TOKENS
    tiles_per_block = N_EXPERTS // HALF_EXPERTS
    n_tiles = nt * tiles_per_block
    assert tiles_per_block == 2 * PEER_STEPS_PER_BLOCK
    assert 2 * (HALF_EXPERTS // N_KEYS) == SUBLANES
    n_steps = n_tiles // 2 + 1
    lb_per_step = PEER_TOKENS // LANES

    prev_step = lambda j: jnp.maximum(j - 1, 0)
    tok_spec = lambda width: pl.BlockSpec(
        (PEER_TOKENS, width), lambda j: (prev_step(j) // PEER_STEPS_PER_BLOCK, 0))
    cur_block = lambda j: jnp.minimum(j // PEER_STEPS_PER_BLOCK, nt - 1)
    key_spec = pl.BlockSpec((HEADS, lb_per_step, N_KEYS, LANES), lambda j: (0, cur_block(j), 0, 0))

    def row_spec(tile_of_step):
        def index(j):
            s = jnp.clip(tile_of_step(j), 0, n_tiles - 1)
            return (0, s // tiles_per_block, (s % tiles_per_block) // 2, 0, 0)
        return pl.BlockSpec((HEADS, lb_per_step, None, SUBLANES, LANES), index)

    rows_a = row_spec(lambda j: 2 * j - 1)
    rows_b = row_spec(lambda j: 2 * j)
    const = lambda shape: pl.BlockSpec(shape, lambda j: (0,) * len(shape), pipeline_mode=pl.Buffered(1))
    return pl.pallas_call(
        functools.partial(_peer_kernel, final),
        grid=(n_steps,),
        in_specs=[
            pl.BlockSpec((D_MODEL // 2, PEER_TOKENS), lambda j: (0, cur_block(j))),
            key_spec, key_spec, rows_a, rows_a, rows_b, rows_b,
            pl.BlockSpec((HALF_EXPERTS, D_MODEL), lambda j: (j % PEER_STEPS_PER_BLOCK, 0)),
            pl.BlockSpec((D_MODEL // 2, 2 * HALF_EXPERTS),
                         lambda j: (0, prev_step(j) % PEER_STEPS_PER_BLOCK)),
            tok_spec(D_MODEL), tok_spec(D_PLE),
            const((1, D_MODEL)), const((D_MODEL, D_MODEL)), const((D_PLE, D_MODEL)), const((1, D_MODEL)),
        ],
        out_specs=tok_spec(D_MODEL),
        out_shape=jax.ShapeDtypeStruct((T_ALL, D_MODEL), F32),
        scratch_shapes=[pltpu.VMEM((D_MODEL, PEER_TOKENS), F32)]
        + [pltpu.VMEM((HEADS, lb_per_step, N_KEYS, LANES), BF16)] * 2
        + [pltpu.VMEM((HALF_EXPERTS, PEER_TOKENS), F32)] * 2
        + [pltpu.VMEM((HALF_EXPERTS, PEER_TOKENS), BF16)] * 2,
        compiler_params=pltpu.CompilerParams(
            dimension_semantics=("arbitrary",), vmem_limit_bytes=VMEM_LIMIT),
        name="peer_dense",
    )(h2t, r2, e2, c1, p1, c1, p1, u_pk, vt_pk, x1, pe, ln3, wpg, wple, lnf)


def _pack_tables_kernel(u_ref, v_ref, upk_ref, vtpk_ref):
    upk_ref[...] = pltpu.bitcast(u_ref[...].astype(BF16), jnp.uint32)
    vtpk_ref[...] = pltpu.bitcast(v_ref[...].T.astype(BF16), jnp.uint32)


def _pack_tables(u_tab, v_tab):
    nb = N_EXPERTS // PACK_EXPERTS
    in_spec = pl.BlockSpec((None, PACK_EXPERTS, D_MODEL), lambda l, e: (l, e, 0))
    return pl.pallas_call(
        _pack_tables_kernel,
        grid=(DEPTH, nb),
        in_specs=[in_spec, in_spec],
        out_specs=[
            pl.BlockSpec((None, PACK_EXPERTS // 2, D_MODEL), lambda l, e: (l, e, 0)),
            pl.BlockSpec((None, D_MODEL // 2, PACK_EXPERTS), lambda l, e: (l, 0, e)),
        ],
        out_shape=[
            jax.ShapeDtypeStruct((DEPTH, N_EXPERTS // 2, D_MODEL), jnp.uint32),
            jax.ShapeDtypeStruct((DEPTH, D_MODEL // 2, N_EXPERTS), jnp.uint32),
        ],
        compiler_params=pltpu.CompilerParams(
            dimension_semantics=("arbitrary", "arbitrary"), vmem_limit_bytes=VMEM_LIMIT),
        name="pack_tables",
    )(u_tab, v_tab)


def kernel(x_prompt, x_sample, state_conv, p_prompt, p_sample, ln1, w_in, ln_v, w_s, b_s, conv_w,
           w_a, w_b, w_o, ln2, w_q, k1, k2, u_tab, v_tab, ln3, w_pg, w_ple, ln_f):
    assert x_prompt.shape == (N_PROMPT_SEQ, PROMPT_LEN, D_MODEL)
    assert x_sample.shape == (N_SAMPLE_SEQ, SAMPLE_LEN, D_MODEL)
    assert u_tab.shape == (DEPTH, N_EXPERTS, D_MODEL)

    x_p = x_prompt.reshape(T_PROMPT, D_MODEL)
    x_s = x_sample.reshape(T_SAMPLE, D_MODEL)
    sample_block = 0
    seqs_per_chunk = CHUNK // SAMPLE_LEN
    eye = jnp.eye(seqs_per_chunk, dtype=F32)
    lnf = ln_f.reshape(1, D_MODEL)
    u_pk, vt_pk = _pack_tables(u_tab, v_tab)
    conv_prompt, conv_sample, v_sample = [], [], []
    y = None

    for i in range(DEPTH):
        row = lambda a: a[i].reshape(1, -1)
        ws_p = w_s[i].astype(BF16)
        corner = w_s[i][:, :SAMPLE_LEN, :SAMPLE_LEN]
        ws_s = (eye[None, :, None, :, None] * corner[:, None, :, None, :]).reshape(
            N_GROUPS, CHUNK, CHUNK).astype(BF16)
        bias_p = jnp.repeat(b_s[i].T, GROUP, axis=1)
        bias_s = jnp.tile(jnp.repeat(b_s[i][:, :SAMPLE_LEN].T, GROUP, axis=1), (seqs_per_chunk, 1))
        shared = (conv_w[i], w_a[i].astype(BF16), w_b[i].astype(BF16), w_o[i].astype(BF16))
        front = (row(ln1), w_in[i].astype(BF16), row(ln_v))
        w_prompt = front + (ws_p, bias_p) + shared
        w_sample = front + (ws_s, bias_s) + shared

        prev = state_conv[i]
        zero = jnp.zeros((N_SAMPLE_SEQ, 1, D_B), F32)
        tap1 = jnp.concatenate([prev[:, 1:2], zero, zero, zero], axis=1).reshape(T_SAMPLE, D_B)
        tap2 = jnp.concatenate([prev[:, 0:1], prev[:, 1:2], zero, zero], axis=1).reshape(T_SAMPLE, D_B)

        x1_p, tail_p = _mixer_prompt(x_p, w_prompt)
        x1_s, v_s, cx_s = _mixer_sample(x_s, sample_block, w_sample, tap1, tap2)
        conv_prompt.append(tail_p)
        conv_sample.append(cx_s.reshape(N_SAMPLE_SEQ, SAMPLE_LEN, D_B)[:, SAMPLE_LEN - 2:])
        v_sample.append(v_s.reshape(N_SAMPLE_SEQ, SAMPLE_LEN, D_A))

        x1 = jnp.concatenate([x1_p, x1_s], axis=0)
        pe = jnp.concatenate([p_prompt[i].reshape(T_PROMPT, D_PLE),
                              p_sample[i].reshape(T_SAMPLE, D_PLE)], axis=0)

        h2t, r2, e2, c1, p1 = _select(x1, row(ln2), w_q[i].T.astype(BF16),
                                      k1[i].astype(BF16), k2[i].astype(BF16))
        final = i == DEPTH - 1
        r2 = r2.reshape(HEADS, T_ALL // LANES, N_KEYS, LANES)
        e2 = e2.reshape(HEADS, T_ALL // LANES, N_KEYS, LANES)
        y = _peer(final, h2t, r2, e2, c1, p1, u_pk[i], vt_pk[i],
                  x1, pe, row(ln3), w_pg[i].astype(BF16), w_ple[i].astype(BF16), lnf)
        x_p = y
        x_s = y
        sample_block = T_PROMPT // T_SAMPLE

    y_prompt = y[:T_PROMPT].reshape(N_PROMPT_SEQ, PROMPT_LEN, D_MODEL)
    y_sample = y[T_PROMPT:].reshape(N_SAMPLE_SEQ, SAMPLE_LEN, D_MODEL)
    return (y_prompt, y_sample, jnp.stack(conv_prompt), jnp.stack(conv_sample), jnp.stack(v_sample))
```

```python
import functools

import jax
import jax.numpy as jnp
from jax import lax
from jax.experimental import pallas as pl
from jax.experimental.pallas import tpu as pltpu

F32 = jnp.float32
BF16 = jnp.bfloat16

D_MODEL = 1024
D_A = 1024
D_B = 1024
D_PLE = 256
CHUNK = 128
N_GROUPS = 8
GROUP = D_A // N_GROUPS
N_KEYS = 128
N_EXPERTS = N_KEYS * N_KEYS
HEADS = 8
D_QUERY = 256
D_HALF = 128
TOPK = 16
EPS = 1e-6
DEPTH = 2

N_PROMPT_SEQ = 8
PROMPT_LEN = 2048
N_SAMPLE_SEQ = 128
SAMPLE_LEN = 4
T_PROMPT = N_PROMPT_SEQ * PROMPT_LEN
T_SAMPLE = N_SAMPLE_SEQ * SAMPLE_LEN
T_ALL = T_PROMPT + T_SAMPLE

LANES = 128
SUBLANES = 8
MIX_ROWS = 256
SEL_TOKENS = 512
PEER_TOKENS = 512
HALF_EXPERTS = 1024
PEER_STEPS_PER_BLOCK = N_EXPERTS // (2 * HALF_EXPERTS)
GATE_ROWS = 4
PACK_EXPERTS = 1024
BF16_ROWS = 16
GATE_KEYS = 64
VMEM_LIMIT = 56 * 1024 * 1024
NEG = -1e30


def _rms(x, g):
    ms = jnp.mean(x * x, axis=-1, keepdims=True)
    return x * lax.rsqrt(ms + EPS) * g


def _sigmoid(x):
    return 1.0 / (1.0 + jnp.exp(-x))


def _dot(a, b):
    return jnp.dot(a, b, preferred_element_type=F32)


def _const_spec(shape, grid_rank):
    zeros = (0,) * len(shape)
    if grid_rank == 1:
        return pl.BlockSpec(shape, lambda i: zeros, pipeline_mode=pl.Buffered(1))
    return pl.BlockSpec(shape, lambda i, j: zeros, pipeline_mode=pl.Buffered(1))


def _mixer_front(x, ln1_ref, win_ref, lnv_ref, ws_ref, bias_ref, yin_ref, rows):
    h = _rms(x, ln1_ref[...]).astype(BF16)

    def proj(k):
        return _dot(h, win_ref[:, k * D_MODEL:(k + 1) * D_MODEL])

    u = jax.nn.gelu(proj(0))
    v = _rms(jax.nn.gelu(proj(1)), lnv_ref[...])
    vb = v.astype(BF16)
    r_i = lax.broadcasted_iota(jnp.int32, (CHUNK, CHUNK), 0)
    c_i = lax.broadcasted_iota(jnp.int32, (CHUNK, CHUNK), 1)
    causal = r_i >= c_i
    for g in range(N_GROUPS):
        w = jnp.where(causal, ws_ref[g], jnp.zeros((), BF16))
        cols = slice(g * GROUP, (g + 1) * GROUP)
        for c in range(rows // CHUNK):
            rws = slice(c * CHUNK, (c + 1) * CHUNK)
            s = _dot(w, vb[rws, cols]) + bias_ref[:, cols]
            yin_ref[rws, cols] = (u[rws, cols] * s).astype(BF16)
    return h, v, proj


def _mixer_back(x, proj, yc, wa_ref, wb_ref, wo_ref, yin_ref):
    ya = _dot(yin_ref[...], wa_ref[...])
    yb = _dot((proj(2) * yc).astype(BF16), wb_ref[...])
    mix = _sigmoid(proj(5)) * ya + _sigmoid(proj(6)) * yb
    return x + _dot(mix.astype(BF16), wo_ref[...])


def _mixer_prompt_kernel(x_ref, ln1_ref, win_ref, lnv_ref, ws_ref, bias_ref, cw_ref,
                         wa_ref, wb_ref, wo_ref, x1_ref, tail_ref, yin_ref, cbuf_ref):
    rows = MIX_ROWS
    x = x_ref[...]
    h, v, proj = _mixer_front(x, ln1_ref, win_ref, lnv_ref, ws_ref, bias_ref, yin_ref, rows)
    cx = proj(3) * proj(4)

    @pl.when(pl.program_id(1) == 0)
    def _():
        cbuf_ref[0:SUBLANES, :] = jnp.zeros((SUBLANES, D_B), F32)

    cbuf_ref[SUBLANES:SUBLANES + rows, :] = cx
    c1 = cbuf_ref[SUBLANES - 1:SUBLANES - 1 + rows, :]
    c2 = cbuf_ref[SUBLANES - 2:SUBLANES - 2 + rows, :]
    yc = cw_ref[0:1, :] * c2 + cw_ref[1:2, :] * c1 + cw_ref[2:3, :] * cx
    cbuf_ref[0:SUBLANES, :] = cx[rows - SUBLANES:rows, :]
    tail_ref[0] = cx[rows - 2:rows, :]
    x1_ref[...] = _mixer_back(x, proj, yc, wa_ref, wb_ref, wo_ref, yin_ref)


def _mixer_sample_kernel(x_ref, ln1_ref, win_ref, lnv_ref, ws_ref, bias_ref, cw_ref,
                         wa_ref, wb_ref, wo_ref, p1_ref, p2_ref,
                         x1_ref, v_ref, cx_ref, yin_ref, cbuf_ref):
    rows = T_SAMPLE
    x = x_ref[...]
    h, v, proj = _mixer_front(x, ln1_ref, win_ref, lnv_ref, ws_ref, bias_ref, yin_ref, rows)
    v_ref[...] = v
    cx = proj(3) * proj(4)
    cx_ref[...] = cx
    cbuf_ref[0:SUBLANES, :] = jnp.zeros((SUBLANES, D_B), F32)
    cbuf_ref[SUBLANES:SUBLANES + rows, :] = cx
    pos = lax.broadcasted_iota(jnp.int32, (rows, D_B), 0) & (SAMPLE_LEN - 1)
    c1 = jnp.where(pos >= 1, cbuf_ref[SUBLANES - 1:SUBLANES - 1 + rows, :], p1_ref[...])
    c2 = jnp.where(pos >= 2, cbuf_ref[SUBLANES - 2:SUBLANES - 2 + rows, :], p2_ref[...])
    yc = cw_ref[0:1, :] * c2 + cw_ref[1:2, :] * c1 + cw_ref[2:3, :] * cx
    x1_ref[...] = _mixer_back(x, proj, yc, wa_ref, wb_ref, wo_ref, yin_ref)


def _mixer_weight_specs(grid_rank):
    cs = functools.partial(_const_spec, grid_rank=grid_rank)
    return [
        cs((1, D_MODEL)),
        cs((D_MODEL, 7 * D_MODEL)),
        cs((1, D_A)),
        cs((N_GROUPS, CHUNK, CHUNK)),
        cs((CHUNK, D_A)),
        cs((3, D_B)),
        cs((D_A, D_MODEL)),
        cs((D_B, D_MODEL)),
        cs((D_MODEL, D_MODEL)),
    ]


def _mixer_prompt(x2d, weights):
    nj = PROMPT_LEN // MIX_ROWS
    return pl.pallas_call(
        _mixer_prompt_kernel,
        grid=(N_PROMPT_SEQ, nj),
        in_specs=[pl.BlockSpec((MIX_ROWS, D_MODEL), lambda b, j: (b * nj + j, 0))]
        + _mixer_weight_specs(2),
        out_specs=[
            pl.BlockSpec((MIX_ROWS, D_MODEL), lambda b, j: (b * nj + j, 0)),
            pl.BlockSpec((1, 2, D_B), lambda b, j: (b, 0, 0)),
        ],
        out_shape=[
            jax.ShapeDtypeStruct((T_PROMPT, D_MODEL), F32),
            jax.ShapeDtypeStruct((N_PROMPT_SEQ, 2, D_B), F32),
        ],
        scratch_shapes=[
            pltpu.VMEM((MIX_ROWS, D_A), BF16),
            pltpu.VMEM((MIX_ROWS + SUBLANES, D_B), F32),
        ],
        compiler_params=pltpu.CompilerParams(
            dimension_semantics=("arbitrary", "arbitrary"), vmem_limit_bytes=VMEM_LIMIT),
        name="mixer_prompt",
    )(x2d, *weights)


def _mixer_sample(x2d, row_block, weights, p1, p2):
    full = lambda shape: pl.BlockSpec(shape, lambda i: (0,) * len(shape))
    return pl.pallas_call(
        _mixer_sample_kernel,
        grid=(1,),
        in_specs=[pl.BlockSpec((T_SAMPLE, D_MODEL), lambda i: (row_block, 0))]
        + _mixer_weight_specs(1)
        + [full((T_SAMPLE, D_B)), full((T_SAMPLE, D_B))],
        out_specs=[full((T_SAMPLE, D_MODEL)), full((T_SAMPLE, D_A)), full((T_SAMPLE, D_B))],
        out_shape=[
            jax.ShapeDtypeStruct((T_SAMPLE, D_MODEL), F32),
            jax.ShapeDtypeStruct((T_SAMPLE, D_A), F32),
            jax.ShapeDtypeStruct((T_SAMPLE, D_B), F32),
        ],
        scratch_shapes=[
            pltpu.VMEM((T_SAMPLE, D_A), BF16),
            pltpu.VMEM((T_SAMPLE + SUBLANES, D_B), F32),
        ],
        compiler_params=pltpu.CompilerParams(
            dimension_semantics=("arbitrary",), vmem_limit_bytes=VMEM_LIMIT),
        name="mixer_sample",
    )(x2d, *weights, p1, p2)


def _cmpx(v, i, j):
    hi = jnp.maximum(v[i], v[j])
    lo = jnp.minimum(v[i], v[j])
    v[i] = hi
    v[j] = lo


def _bitonic_merge16(v):
    j = TOPK // 2
    while j >= 1:
        for i in range(TOPK):
            l = i ^ j
            if l > i:
                _cmpx(v, i, l)
        j //= 2


def _sort16(v):
    k = 2
    while k <= TOPK:
        j = k // 2
        while j >= 1:
            for i in range(TOPK):
                l = i ^ j
                if l > i:
                    if (i & k) == 0:
                        _cmpx(v, i, l)
                    else:
                        _cmpx(v, l, i)
            j //= 2
        k *= 2


def _top16(s):
    v = [s[k] for k in range(TOPK)]
    _sort16(v)
    for shift in (4, 2, 1):
        w = [pltpu.roll(v[k], shift, 0) for k in range(TOPK)]
        v = [jnp.maximum(v[k], w[TOPK - 1 - k]) for k in range(TOPK)]
        _bitonic_merge16(v)
    return v


def _sublane_allreduce(x, op):
    for shift in (4, 2, 1):
        x = op(x, pltpu.roll(x, shift, 0))
    return x


def _on_sublanes(vals, sub):
    out = vals[SUBLANES - 1]
    for r in range(SUBLANES - 2, -1, -1):
        out = jnp.where(sub == r, vals[r], out)
    return out


def _select_kernel(x1_ref, ln2_ref, wqt_ref, k1_ref, k2_ref,
                   h2t_ref, r2_ref, e2_ref, c1_ref, p1_ref, s1_scr, s2_scr):
    tokens = SEL_TOKENS
    h2 = _rms(x1_ref[...], ln2_ref[...])
    h2t = h2.T.astype(BF16)
    h2t_ref[...] = pltpu.bitcast(h2t, jnp.uint32)
    qt = _dot(wqt_ref[...], h2t)
    for h in range(HEADS):
        q1 = qt[h * D_QUERY:h * D_QUERY + D_HALF].astype(BF16)
        q2 = qt[h * D_QUERY + D_HALF:(h + 1) * D_QUERY].astype(BF16)
        s1_scr[h] = _dot(k1_ref[...], q1).reshape(TOPK, SUBLANES, tokens)
        s2_scr[h] = _dot(k2_ref[...], q2).reshape(TOPK, SUBLANES, tokens)

    sub = lax.broadcasted_iota(jnp.int32, (SUBLANES, LANES), 0)

    def lane_block(lb, carry):
        lanes = pl.ds(pl.multiple_of(lb * LANES, LANES), LANES)
        for h in range(HEADS):
            s1 = s1_scr[h, :, :, lanes]
            s2 = s2_scr[h, :, :, lanes]
            t1 = _top16(s1)
            t2 = _top16(s2)
            t2lo = _on_sublanes(t2[0:8], sub)
            t2hi = _on_sublanes(t2[8:16], sub)
            t1hi = _on_sublanes(t1[8:16], sub)
            cands = [t1[0] + t2lo, t1[0] + t2hi, t1[1] + t2lo]
            for a, nb in ((2, 5), (3, 4), (4, 3), (5, 2), (6, 2), (7, 2)):
                cands.append(jnp.where(sub < nb, t1[a] + t2lo, NEG))
            cands.append(t1hi + t2[0])
            cur = list(cands)
            tops = []
            for it in range(TOPK + 1):
                m = functools.reduce(jnp.maximum, cur)
                m = _sublane_allreduce(m, jnp.maximum)
                tops.append(m)
                if it < TOPK:
                    cur = [jnp.where(c == m, NEG, c) for c in cur]
            tau = 0.5 * (tops[TOPK - 1] + tops[TOPK])
            z = functools.reduce(
                jnp.add, [jnp.where(c >= tau, jnp.exp(c - tops[0]), 0.0) for c in cands])
            z = _sublane_allreduce(z, jnp.add)
            inv_z = 1.0 / z
            p1 = jnp.where(s1 >= t1[TOPK - 1][None], jnp.exp(s1 - t1[0][None]) * inv_z[None], 0.0)
            e2 = jnp.where(s2 >= t2[TOPK - 1][None], jnp.exp(s2 - t2[0][None]), 0.0)
            th = tau[None] - s1
            rank2 = jnp.zeros_like(s2)
            count1 = jnp.zeros_like(s1)
            for b in range(TOPK):
                rank2 = rank2 + jnp.where(t2[b][None] > s2, 1.0, 0.0)
                count1 = count1 + jnp.where(t2[b][None] >= th, 1.0, 0.0)
            r2_ref[h, lb] = rank2
            e2_ref[h, lb] = e2
            c1_ref[h, lb] = count1
            p1_ref[h, lb] = p1
        return carry

    lax.fori_loop(0, tokens // LANES, lane_block, 0)


def _select(x1, ln2, wqt, k1, k2):
    nb = T_ALL // SEL_TOKENS
    lb_per_step = SEL_TOKENS // LANES
    n_lb = T_ALL // LANES
    key_shape = jax.ShapeDtypeStruct((HEADS, n_lb, TOPK, SUBLANES, LANES), F32)
    key_spec = pl.BlockSpec((HEADS, lb_per_step, TOPK, SUBLANES, LANES), lambda i: (0, i, 0, 0, 0))
    scr = pltpu.VMEM((HEADS, TOPK, SUBLANES, SEL_TOKENS), F32)
    return pl.pallas_call(
        _select_kernel,
        grid=(nb,),
        in_specs=[
            pl.BlockSpec((SEL_TOKENS, D_MODEL), lambda i: (i, 0)),
            _const_spec((1, D_MODEL), 1),
            _const_spec((HEADS * D_QUERY, D_MODEL), 1),
            _const_spec((N_KEYS, D_HALF), 1),
            _const_spec((N_KEYS, D_HALF), 1),
        ],
        out_specs=[pl.BlockSpec((D_MODEL // 2, SEL_TOKENS), lambda i: (0, i))] + [key_spec] * 4,
        out_shape=[jax.ShapeDtypeStruct((D_MODEL // 2, T_ALL), jnp.uint32)] + [key_shape] * 4,
        scratch_shapes=[scr, scr],
        compiler_params=pltpu.CompilerParams(
            dimension_semantics=("arbitrary",), vmem_limit_bytes=VMEM_LIMIT),
        name="peer_select",
    )(x1, ln2, wqt, k1, k2)


def _peer_kernel(final, h2t_ref, r2_ref, e2_ref, c1a_ref, p1a_ref, c1b_ref, p1b_ref, u_ref, vt_ref,
                 x1_ref, pe_ref, ln3_ref, wpg_ref, wple_ref, lnf_ref, o_ref,
                 acc_ref, r2_scr, e2_scr, at0_ref, at1_ref, ga0_ref, ga1_ref):
    j = pl.program_id(0)
    phase = j % PEER_STEPS_PER_BLOCK
    rows_half = HALF_EXPERTS // N_KEYS
    vregs = GATE_KEYS // BF16_ROWS

    def load_keys():
        for h in range(HEADS):
            r2_scr[h] = r2_ref[h].astype(BF16)
            e2_scr[h] = e2_ref[h].astype(BF16)

    @pl.when(j == 0)
    def _():
        at1_ref[...] = jnp.zeros_like(at1_ref)
        ga0_ref[...] = jnp.zeros_like(ga0_ref)
        ga1_ref[...] = jnp.zeros_like(ga1_ref)
        load_keys()

    @pl.when((phase == 1) | (j == 0))
    def _():
        acc_ref[...] = jnp.zeros_like(acc_ref)

    def lane_bcast(ref, h, lb, r):
        return jnp.broadcast_to(ref[h, lb, r:r + 1, :], (BF16_ROWS, LANES)).astype(BF16)

    def gate_tile(at_ref, ga_ref, c1_ref, p1_ref, row0, lb, rg, kq):
        lanes = slice(lb * LANES, (lb + 1) * LANES)
        keys = [slice(kq * GATE_KEYS + k * BF16_ROWS, kq * GATE_KEYS + (k + 1) * BF16_ROWS)
                for k in range(vregs)]
        key_rows = range(rg * GATE_ROWS, (rg + 1) * GATE_ROWS)
        g = {r: [None] * vregs for r in key_rows}
        for h in range(HEADS):
            r2 = [r2_scr[h, lb, ks, :] for ks in keys]
            e2 = [e2_scr[h, lb, ks, :] for ks in keys]
            for r in key_rows:
                c = lane_bcast(c1_ref, h, lb, row0 + r)
                p = lane_bcast(p1_ref, h, lb, row0 + r)
                for k in range(vregs):
                    term = jnp.where(r2[k] < c, e2[k], jnp.zeros_like(e2[k])) * p
                    g[r][k] = term if h == 0 else g[r][k] + term
        for r in key_rows:
            for k in range(vregs):
                rows = slice(r * N_KEYS + keys[k].start, r * N_KEYS + keys[k].stop)
                a = at_ref[rows, lanes].astype(BF16)
                ga_ref[rows, lanes] = g[r][k] * jax.nn.gelu(a)

    def half_step(half, at_new, at_cur, ga_new, ga_old, c1_ref, p1_ref, row0):
        u_rows = slice(half * HALF_EXPERTS // 2, (half + 1) * HALF_EXPERTS // 2)
        v_cols = slice(half * HALF_EXPERTS, (half + 1) * HALF_EXPERTS)
        at_new[...] = _dot(pltpu.bitcast(u_ref[u_rows, :], BF16), pltpu.bitcast(h2t_ref[...], BF16))
        for lb in range(PEER_TOKENS // LANES):
            for rg in range(rows_half // GATE_ROWS):
                for kq in range(N_KEYS // GATE_KEYS):
                    gate_tile(at_cur, ga_new, c1_ref, p1_ref, row0, lb, rg, kq)
        acc_ref[...] += _dot(pltpu.bitcast(vt_ref[:, v_cols], BF16), ga_old[...])

    half_step(0, at0_ref, at1_ref, ga1_ref, ga0_ref, c1a_ref, p1a_ref, rows_half % SUBLANES)

    @pl.when(phase == 0)
    def _():
        load_keys()

    half_step(1, at1_ref, at0_ref, ga0_ref, ga1_ref, c1b_ref, p1b_ref, 0)

    @pl.when((phase == 0) & (j > 0))
    def _():
        x2 = x1_ref[...] + acc_ref[...].T
        h3 = _rms(x2, ln3_ref[...]).astype(BF16)
        gate = _sigmoid(_dot(h3, wpg_ref[...]))
        x3 = x2 + gate * _dot(pe_ref[...].astype(BF16), wple_ref[...])
        o_ref[...] = _rms(x3, lnf_ref[...]) if final else x3


def _peer(final, h2t, r2, e2, c1, p1, u_pk, vt_pk, x1, pe, ln3, wpg, wple, lnf):
    nt = T_ALL // PEER_TOKENS
    tiles_per_block = N_EXPERTS // HALF_EXPERTS
    n_tiles = nt * tiles_per_block
    assert tiles_per_block == 2 * PEER_STEPS_PER_BLOCK
    rows_half = HALF_EXPERTS // N_KEYS
    assert SUBLANES % rows_half == 0
    n_steps = n_tiles // 2 + 1
    lb_per_step = PEER_TOKENS // LANES

    prev_step = lambda j: jnp.maximum(j - 1, 0)
    tok_spec = lambda width: pl.BlockSpec(
        (PEER_TOKENS, width), lambda j: (prev_step(j) // PEER_STEPS_PER_BLOCK, 0))
    cur_block = lambda j: jnp.minimum(j // PEER_STEPS_PER_BLOCK, nt - 1)
    key_spec = pl.BlockSpec((HEADS, lb_per_step, N_KEYS, LANES), lambda j: (0, cur_block(j), 0, 0))

    def row_spec(tile_of_step):
        def index(j):
            s = jnp.clip(tile_of_step(j), 0, n_tiles - 1)
            return (0, s // tiles_per_block, (s % tiles_per_block) * rows_half // SUBLANES, 0, 0)
        return pl.BlockSpec((HEADS, lb_per_step, None, SUBLANES, LANES), index)

    rows_a = row_spec(lambda j: 2 * j - 1)
    rows_b = row_spec(lambda j: 2 * j)
    const = lambda shape: pl.BlockSpec(shape, lambda j: (0,) * len(shape), pipeline_mode=pl.Buffered(1))
    return pl.pallas_call(
        functools.partial(_peer_kernel, final),
        grid=(n_steps,),
        in_specs=[
            pl.BlockSpec((D_MODEL // 2, PEER_TOKENS), lambda j: (0, cur_block(j))),
            key_spec, key_spec, rows_a, rows_a, rows_b, rows_b,
            pl.BlockSpec((HALF_EXPERTS, D_MODEL), lambda j: (j % PEER_STEPS_PER_BLOCK, 0)),
            pl.BlockSpec((D_MODEL // 2, 2 * HALF_EXPERTS),
                         lambda j: (0, prev_step(j) % PEER_STEPS_PER_BLOCK)),
            tok_spec(D_MODEL), tok_spec(D_PLE),
            const((1, D_MODEL)), const((D_MODEL, D_MODEL)), const((D_PLE, D_MODEL)), const((1, D_MODEL)),
        ],
        out_specs=tok_spec(D_MODEL),
        out_shape=jax.ShapeDtypeStruct((T_ALL, D_MODEL), F32),
        scratch_shapes=[pltpu.VMEM((D_MODEL, PEER_TOKENS), F32)]
        + [pltpu.VMEM((HEADS, lb_per_step, N_KEYS, LANES), BF16)] * 2
        + [pltpu.VMEM((HALF_EXPERTS, PEER_TOKENS), F32)] * 2
        + [pltpu.VMEM((HALF_EXPERTS, PEER_TOKENS), BF16)] * 2,
        compiler_params=pltpu.CompilerParams(
            dimension_semantics=("arbitrary",), vmem_limit_bytes=VMEM_LIMIT),
        name="peer_dense",
    )(h2t, r2, e2, c1, p1, c1, p1, u_pk, vt_pk, x1, pe, ln3, wpg, wple, lnf)


def _pack_tables_kernel(u_ref, v_ref, upk_ref, vtpk_ref):
    upk_ref[...] = pltpu.bitcast(u_ref[...].astype(BF16), jnp.uint32)
    vtpk_ref[...] = pltpu.bitcast(v_ref[...].T.astype(BF16), jnp.uint32)


def _pack_tables(u_tab, v_tab):
    nb = N_EXPERTS // PACK_EXPERTS
    in_spec = pl.BlockSpec((None, PACK_EXPERTS, D_MODEL), lambda l, e: (l, e, 0))
    return pl.pallas_call(
        _pack_tables_kernel,
        grid=(DEPTH, nb),
        in_specs=[in_spec, in_spec],
        out_specs=[
            pl.BlockSpec((None, PACK_EXPERTS // 2, D_MODEL), lambda l, e: (l, e, 0)),
            pl.BlockSpec((None, D_MODEL // 2, PACK_EXPERTS), lambda l, e: (l, 0, e)),
        ],
        out_shape=[
            jax.ShapeDtypeStruct((DEPTH, N_EXPERTS // 2, D_MODEL), jnp.uint32),
            jax.ShapeDtypeStruct((DEPTH, D_MODEL // 2, N_EXPERTS), jnp.uint32),
        ],
        compiler_params=pltpu.CompilerParams(
            dimension_semantics=("arbitrary", "arbitrary"), vmem_limit_bytes=VMEM_LIMIT),
        name="pack_tables",
    )(u_tab, v_tab)


def kernel(x_prompt, x_sample, state_conv, p_prompt, p_sample, ln1, w_in, ln_v, w_s, b_s, conv_w,
           w_a, w_b, w_o, ln2, w_q, k1, k2, u_tab, v_tab, ln3, w_pg, w_ple, ln_f):
    assert x_prompt.shape == (N_PROMPT_SEQ, PROMPT_LEN, D_MODEL)
    assert x_sample.shape == (N_SAMPLE_SEQ, SAMPLE_LEN, D_MODEL)
    assert u_tab.shape == (DEPTH, N_EXPERTS, D_MODEL)

    x_p = x_prompt.reshape(T_PROMPT, D_MODEL)
    x_s = x_sample.reshape(T_SAMPLE, D_MODEL)
    sample_block = 0
    seqs_per_chunk = CHUNK // SAMPLE_LEN
    eye = jnp.eye(seqs_per_chunk, dtype=F32)
    lnf = ln_f.reshape(1, D_MODEL)
    u_pk, vt_pk = _pack_tables(u_tab, v_tab)
    conv_prompt, conv_sample, v_sample = [], [], []
    y = None

    for i in range(DEPTH):
        row = lambda a: a[i].reshape(1, -1)
        ws_p = w_s[i].astype(BF16)
        corner = w_s[i][:, :SAMPLE_LEN, :SAMPLE_LEN]
        ws_s = (eye[None, :, None, :, None] * corner[:, None, :, None, :]).reshape(
            N_GROUPS, CHUNK, CHUNK).astype(BF16)
        bias_p = jnp.repeat(b_s[i].T, GROUP, axis=1)
        bias_s = jnp.tile(jnp.repeat(b_s[i][:, :SAMPLE_LEN].T, GROUP, axis=1), (seqs_per_chunk, 1))
        shared = (conv_w[i], w_a[i].astype(BF16), w_b[i].astype(BF16), w_o[i].astype(BF16))
        front = (row(ln1), w_in[i].astype(BF16), row(ln_v))
        w_prompt = front + (ws_p, bias_p) + shared
        w_sample = front + (ws_s, bias_s) + shared

        prev = state_conv[i]
        zero = jnp.zeros((N_SAMPLE_SEQ, 1, D_B), F32)
        tap1 = jnp.concatenate([prev[:, 1:2], zero, zero, zero], axis=1).reshape(T_SAMPLE, D_B)
        tap2 = jnp.concatenate([prev[:, 0:1], prev[:, 1:2], zero, zero], axis=1).reshape(T_SAMPLE, D_B)

        x1_p, tail_p = _mixer_prompt(x_p, w_prompt)
        x1_s, v_s, cx_s = _mixer_sample(x_s, sample_block, w_sample, tap1, tap2)
        conv_prompt.append(tail_p)
        conv_sample.append(cx_s.reshape(N_SAMPLE_SEQ, SAMPLE_LEN, D_B)[:, SAMPLE_LEN - 2:])
        v_sample.append(v_s.reshape(N_SAMPLE_SEQ, SAMPLE_LEN, D_A))

        x1 = jnp.concatenate([x1_p, x1_s], axis=0)
        pe = jnp.concatenate([p_prompt[i].reshape(T_PROMPT, D_PLE),
                              p_sample[i].reshape(T_SAMPLE, D_PLE)], axis=0)

        h2t, r2, e2, c1, p1 = _select(x1, row(ln2), w_q[i].T.astype(BF16),
                                      k1[i].astype(BF16), k2[i].astype(BF16))
        final = i == DEPTH - 1
        r2 = r2.reshape(HEADS, T_ALL // LANES, N_KEYS, LANES)
        e2 = e2.reshape(HEADS, T_ALL // LANES, N_KEYS, LANES)
        y = _peer(final, h2t, r2, e2, c1, p1, u_pk[i], vt_pk[i],
                  x1, pe, row(ln3), w_pg[i].astype(BF16), w_ple[i].astype(BF16), lnf)
        x_p = y
        x_s = y
        sample_block = T_PROMPT // T_SAMPLE

    y_prompt = y[:T_PROMPT].reshape(N_PROMPT_SEQ, PROMPT_LEN, D_MODEL)
    y_sample = y[T_PROMPT:].reshape(N_SAMPLE_SEQ, SAMPLE_LEN, D_MODEL)
    return (y_prompt, y_sample, jnp.stack(conv_prompt), jnp.stack(conv_sample), jnp.stack(v_sample))
```

```python
import functools

import jax
import jax.numpy as jnp
from jax import lax
from jax.experimental import pallas as pl
from jax.experimental.pallas import tpu as pltpu

F32 = jnp.float32
BF16 = jnp.bfloat16

D_MODEL = 1024
D_A = 1024
D_B = 1024
D_PLE = 256
CHUNK = 128
N_GROUPS = 8
GROUP = D_A // N_GROUPS
N_KEYS = 128
N_EXPERTS = N_KEYS * N_KEYS
HEADS = 8
D_QUERY = 256
D_HALF = 128
TOPK = 16
EPS = 1e-6
DEPTH = 2

N_PROMPT_SEQ = 8
PROMPT_LEN = 2048
N_SAMPLE_SEQ = 128
SAMPLE_LEN = 4
T_PROMPT = N_PROMPT_SEQ * PROMPT_LEN
T_SAMPLE = N_SAMPLE_SEQ * SAMPLE_LEN
T_ALL = T_PROMPT + T_SAMPLE

LANES = 128
SUBLANES = 8
MIX_ROWS = 256
SEL_TOKENS = 512
PEER_TOKENS = 512
HALF_EXPERTS = 1024
PEER_STEPS_PER_BLOCK = N_EXPERTS // (2 * HALF_EXPERTS)
GATE_ROWS = 4
MXU_SPLIT = 1
PACK_EXPERTS = 1024
BF16_ROWS = 16
GATE_KEYS = 64
VMEM_LIMIT = 56 * 1024 * 1024
NEG = -1e30


def _rms(x, g):
    ms = jnp.mean(x * x, axis=-1, keepdims=True)
    return x * lax.rsqrt(ms + EPS) * g


def _sigmoid(x):
    return 1.0 / (1.0 + jnp.exp(-x))


def _dot(a, b):
    return jnp.dot(a, b, preferred_element_type=F32)


def _const_spec(shape, grid_rank):
    zeros = (0,) * len(shape)
    if grid_rank == 1:
        return pl.BlockSpec(shape, lambda i: zeros, pipeline_mode=pl.Buffered(1))
    return pl.BlockSpec(shape, lambda i, j: zeros, pipeline_mode=pl.Buffered(1))


def _mixer_front(x, ln1_ref, win_ref, lnv_ref, ws_ref, bias_ref, yin_ref, rows):
    h = _rms(x, ln1_ref[...]).astype(BF16)

    def proj(k):
        return _dot(h, win_ref[:, k * D_MODEL:(k + 1) * D_MODEL])

    u = jax.nn.gelu(proj(0))
    v = _rms(jax.nn.gelu(proj(1)), lnv_ref[...])
    vb = v.astype(BF16)
    r_i = lax.broadcasted_iota(jnp.int32, (CHUNK, CHUNK), 0)
    c_i = lax.broadcasted_iota(jnp.int32, (CHUNK, CHUNK), 1)
    causal = r_i >= c_i
    for g in range(N_GROUPS):
        w = jnp.where(causal, ws_ref[g], jnp.zeros((), BF16))
        cols = slice(g * GROUP, (g + 1) * GROUP)
        for c in range(rows // CHUNK):
            rws = slice(c * CHUNK, (c + 1) * CHUNK)
            s = _dot(w, vb[rws, cols]) + bias_ref[:, cols]
            yin_ref[rws, cols] = (u[rws, cols] * s).astype(BF16)
    return h, v, proj


def _mixer_back(x, proj, yc, wa_ref, wb_ref, wo_ref, yin_ref):
    ya = _dot(yin_ref[...], wa_ref[...])
    yb = _dot((proj(2) * yc).astype(BF16), wb_ref[...])
    mix = _sigmoid(proj(5)) * ya + _sigmoid(proj(6)) * yb
    return x + _dot(mix.astype(BF16), wo_ref[...])


def _mixer_prompt_kernel(x_ref, ln1_ref, win_ref, lnv_ref, ws_ref, bias_ref, cw_ref,
                         wa_ref, wb_ref, wo_ref, x1_ref, tail_ref, yin_ref, cbuf_ref):
    rows = MIX_ROWS
    x = x_ref[...]
    h, v, proj = _mixer_front(x, ln1_ref, win_ref, lnv_ref, ws_ref, bias_ref, yin_ref, rows)
    cx = proj(3) * proj(4)

    @pl.when(pl.program_id(1) == 0)
    def _():
        cbuf_ref[0:SUBLANES, :] = jnp.zeros((SUBLANES, D_B), F32)

    cbuf_ref[SUBLANES:SUBLANES + rows, :] = cx
    c1 = cbuf_ref[SUBLANES - 1:SUBLANES - 1 + rows, :]
    c2 = cbuf_ref[SUBLANES - 2:SUBLANES - 2 + rows, :]
    yc = cw_ref[0:1, :] * c2 + cw_ref[1:2, :] * c1 + cw_ref[2:3, :] * cx
    cbuf_ref[0:SUBLANES, :] = cx[rows - SUBLANES:rows, :]
    tail_ref[0] = cx[rows - 2:rows, :]
    x1_ref[...] = _mixer_back(x, proj, yc, wa_ref, wb_ref, wo_ref, yin_ref)


def _mixer_sample_kernel(x_ref, ln1_ref, win_ref, lnv_ref, ws_ref, bias_ref, cw_ref,
                         wa_ref, wb_ref, wo_ref, p1_ref, p2_ref,
                         x1_ref, v_ref, cx_ref, yin_ref, cbuf_ref):
    rows = T_SAMPLE
    x = x_ref[...]
    h, v, proj = _mixer_front(x, ln1_ref, win_ref, lnv_ref, ws_ref, bias_ref, yin_ref, rows)
    v_ref[...] = v
    cx = proj(3) * proj(4)
    cx_ref[...] = cx
    cbuf_ref[0:SUBLANES, :] = jnp.zeros((SUBLANES, D_B), F32)
    cbuf_ref[SUBLANES:SUBLANES + rows, :] = cx
    pos = lax.broadcasted_iota(jnp.int32, (rows, D_B), 0) & (SAMPLE_LEN - 1)
    c1 = jnp.where(pos >= 1, cbuf_ref[SUBLANES - 1:SUBLANES - 1 + rows, :], p1_ref[...])
    c2 = jnp.where(pos >= 2, cbuf_ref[SUBLANES - 2:SUBLANES - 2 + rows, :], p2_ref[...])
    yc = cw_ref[0:1, :] * c2 + cw_ref[1:2, :] * c1 + cw_ref[2:3, :] * cx
    x1_ref[...] = _mixer_back(x, proj, yc, wa_ref, wb_ref, wo_ref, yin_ref)


def _mixer_weight_specs(grid_rank):
    cs = functools.partial(_const_spec, grid_rank=grid_rank)
    return [
        cs((1, D_MODEL)),
        cs((D_MODEL, 7 * D_MODEL)),
        cs((1, D_A)),
        cs((N_GROUPS, CHUNK, CHUNK)),
        cs((CHUNK, D_A)),
        cs((3, D_B)),
        cs((D_A, D_MODEL)),
        cs((D_B, D_MODEL)),
        cs((D_MODEL, D_MODEL)),
    ]


def _mixer_prompt(x2d, weights):
    nj = PROMPT_LEN // MIX_ROWS
    return pl.pallas_call(
        _mixer_prompt_kernel,
        grid=(N_PROMPT_SEQ, nj),
        in_specs=[pl.BlockSpec((MIX_ROWS, D_MODEL), lambda b, j: (b * nj + j, 0))]
        + _mixer_weight_specs(2),
        out_specs=[
            pl.BlockSpec((MIX_ROWS, D_MODEL), lambda b, j: (b * nj + j, 0)),
            pl.BlockSpec((1, 2, D_B), lambda b, j: (b, 0, 0)),
        ],
        out_shape=[
            jax.ShapeDtypeStruct((T_PROMPT, D_MODEL), F32),
            jax.ShapeDtypeStruct((N_PROMPT_SEQ, 2, D_B), F32),
        ],
        scratch_shapes=[
            pltpu.VMEM((MIX_ROWS, D_A), BF16),
            pltpu.VMEM((MIX_ROWS + SUBLANES, D_B), F32),
        ],
        compiler_params=pltpu.CompilerParams(
            dimension_semantics=("arbitrary", "arbitrary"), vmem_limit_bytes=VMEM_LIMIT),
        name="mixer_prompt",
    )(x2d, *weights)


def _mixer_sample(x2d, row_block, weights, p1, p2):
    full = lambda shape: pl.BlockSpec(shape, lambda i: (0,) * len(shape))
    return pl.pallas_call(
        _mixer_sample_kernel,
        grid=(1,),
        in_specs=[pl.BlockSpec((T_SAMPLE, D_MODEL), lambda i: (row_block, 0))]
        + _mixer_weight_specs(1)
        + [full((T_SAMPLE, D_B)), full((T_SAMPLE, D_B))],
        out_specs=[full((T_SAMPLE, D_MODEL)), full((T_SAMPLE, D_A)), full((T_SAMPLE, D_B))],
        out_shape=[
            jax.ShapeDtypeStruct((T_SAMPLE, D_MODEL), F32),
            jax.ShapeDtypeStruct((T_SAMPLE, D_A), F32),
            jax.ShapeDtypeStruct((T_SAMPLE, D_B), F32),
        ],
        scratch_shapes=[
            pltpu.VMEM((T_SAMPLE, D_A), BF16),
            pltpu.VMEM((T_SAMPLE + SUBLANES, D_B), F32),
        ],
        compiler_params=pltpu.CompilerParams(
            dimension_semantics=("arbitrary",), vmem_limit_bytes=VMEM_LIMIT),
        name="mixer_sample",
    )(x2d, *weights, p1, p2)


def _cmpx(v, i, j):
    hi = jnp.maximum(v[i], v[j])
    lo = jnp.minimum(v[i], v[j])
    v[i] = hi
    v[j] = lo


def _bitonic_merge16(v):
    j = TOPK // 2
    while j >= 1:
        for i in range(TOPK):
            l = i ^ j
            if l > i:
                _cmpx(v, i, l)
        j //= 2


def _sort16(v):
    k = 2
    while k <= TOPK:
        j = k // 2
        while j >= 1:
            for i in range(TOPK):
                l = i ^ j
                if l > i:
                    if (i & k) == 0:
                        _cmpx(v, i, l)
                    else:
                        _cmpx(v, l, i)
            j //= 2
        k *= 2


def _top16(s):
    v = [s[k] for k in range(TOPK)]
    _sort16(v)
    for shift in (4, 2, 1):
        w = [pltpu.roll(v[k], shift, 0) for k in range(TOPK)]
        v = [jnp.maximum(v[k], w[TOPK - 1 - k]) for k in range(TOPK)]
        _bitonic_merge16(v)
    return v


def _sublane_allreduce(x, op):
    for shift in (4, 2, 1):
        x = op(x, pltpu.roll(x, shift, 0))
    return x


def _on_sublanes(vals, sub):
    out = vals[SUBLANES - 1]
    for r in range(SUBLANES - 2, -1, -1):
        out = jnp.where(sub == r, vals[r], out)
    return out


def _select_kernel(x1_ref, ln2_ref, wqt_ref, k1_ref, k2_ref,
                   h2t_ref, r2_ref, e2_ref, c1_ref, p1_ref, s1_scr, s2_scr):
    tokens = SEL_TOKENS
    h2 = _rms(x1_ref[...], ln2_ref[...])
    h2t = h2.T.astype(BF16)
    h2t_ref[...] = pltpu.bitcast(h2t, jnp.uint32)
    qt = _dot(wqt_ref[...], h2t)
    for h in range(HEADS):
        q1 = qt[h * D_QUERY:h * D_QUERY + D_HALF].astype(BF16)
        q2 = qt[h * D_QUERY + D_HALF:(h + 1) * D_QUERY].astype(BF16)
        s1_scr[h] = _dot(k1_ref[...], q1).reshape(TOPK, SUBLANES, tokens)
        s2_scr[h] = _dot(k2_ref[...], q2).reshape(TOPK, SUBLANES, tokens)

    sub = lax.broadcasted_iota(jnp.int32, (SUBLANES, LANES), 0)

    def lane_block(lb, carry):
        lanes = pl.ds(pl.multiple_of(lb * LANES, LANES), LANES)
        for h in range(HEADS):
            s1 = s1_scr[h, :, :, lanes]
            s2 = s2_scr[h, :, :, lanes]
            t1 = _top16(s1)
            t2 = _top16(s2)
            t2lo = _on_sublanes(t2[0:8], sub)
            t2hi = _on_sublanes(t2[8:16], sub)
            t1hi = _on_sublanes(t1[8:16], sub)
            cands = [t1[0] + t2lo, t1[0] + t2hi, t1[1] + t2lo]
            for a, nb in ((2, 5), (3, 4), (4, 3), (5, 2), (6, 2), (7, 2)):
                cands.append(jnp.where(sub < nb, t1[a] + t2lo, NEG))
            cands.append(t1hi + t2[0])
            cur = list(cands)
            tops = []
            for it in range(TOPK + 1):
                m = functools.reduce(jnp.maximum, cur)
                m = _sublane_allreduce(m, jnp.maximum)
                tops.append(m)
                if it < TOPK:
                    cur = [jnp.where(c == m, NEG, c) for c in cur]
            tau = 0.5 * (tops[TOPK - 1] + tops[TOPK])
            z = functools.reduce(
                jnp.add, [jnp.where(c >= tau, jnp.exp(c - tops[0]), 0.0) for c in cands])
            z = _sublane_allreduce(z, jnp.add)
            inv_z = 1.0 / z
            p1 = jnp.where(s1 >= t1[TOPK - 1][None], jnp.exp(s1 - t1[0][None]) * inv_z[None], 0.0)
            e2 = jnp.where(s2 >= t2[TOPK - 1][None], jnp.exp(s2 - t2[0][None]), 0.0)
            th = tau[None] - s1
            rank2 = jnp.zeros_like(s2)
            count1 = jnp.zeros_like(s1)
            for b in range(TOPK):
                rank2 = rank2 + jnp.where(t2[b][None] > s2, 1.0, 0.0)
                count1 = count1 + jnp.where(t2[b][None] >= th, 1.0, 0.0)
            r2_ref[h, lb] = rank2
            e2_ref[h, lb] = e2
            c1_ref[h, lb] = count1
            p1_ref[h, lb] = p1
        return carry

    lax.fori_loop(0, tokens // LANES, lane_block, 0)


def _select(x1, ln2, wqt, k1, k2):
    nb = T_ALL // SEL_TOKENS
    lb_per_step = SEL_TOKENS // LANES
    n_lb = T_ALL // LANES
    key_shape = jax.ShapeDtypeStruct((HEADS, n_lb, TOPK, SUBLANES, LANES), F32)
    key_spec = pl.BlockSpec((HEADS, lb_per_step, TOPK, SUBLANES, LANES), lambda i: (0, i, 0, 0, 0))
    scr = pltpu.VMEM((HEADS, TOPK, SUBLANES, SEL_TOKENS), F32)
    return pl.pallas_call(
        _select_kernel,
        grid=(nb,),
        in_specs=[
            pl.BlockSpec((SEL_TOKENS, D_MODEL), lambda i: (i, 0)),
            _const_spec((1, D_MODEL), 1),
            _const_spec((HEADS * D_QUERY, D_MODEL), 1),
            _const_spec((N_KEYS, D_HALF), 1),
            _const_spec((N_KEYS, D_HALF), 1),
        ],
        out_specs=[pl.BlockSpec((D_MODEL // 2, SEL_TOKENS), lambda i: (0, i))] + [key_spec] * 4,
        out_shape=[jax.ShapeDtypeStruct((D_MODEL // 2, T_ALL), jnp.uint32)] + [key_shape] * 4,
        scratch_shapes=[scr, scr],
        compiler_params=pltpu.CompilerParams(
            dimension_semantics=("arbitrary",), vmem_limit_bytes=VMEM_LIMIT),
        name="peer_select",
    )(x1, ln2, wqt, k1, k2)


def _peer_kernel(final, h2t_ref, r2_ref, e2_ref, c1a_ref, p1a_ref, c1b_ref, p1b_ref, u_ref, vt_ref,
                 x1_ref, pe_ref, ln3_ref, wpg_ref, wple_ref, lnf_ref, o_ref,
                 acc_ref, r2_scr, e2_scr, at0_ref, at1_ref, ga0_ref, ga1_ref):
    j = pl.program_id(0)
    phase = j % PEER_STEPS_PER_BLOCK
    rows_half = HALF_EXPERTS // N_KEYS
    vregs = GATE_KEYS // BF16_ROWS

    def load_keys():
        for h in range(HEADS):
            r2_scr[h] = r2_ref[h].astype(BF16)
            e2_scr[h] = e2_ref[h].astype(BF16)

    @pl.when(j == 0)
    def _():
        at1_ref[...] = jnp.zeros_like(at1_ref)
        ga0_ref[...] = jnp.zeros_like(ga0_ref)
        ga1_ref[...] = jnp.zeros_like(ga1_ref)
        load_keys()

    @pl.when((phase == 1) | (j == 0))
    def _():
        acc_ref[...] = jnp.zeros_like(acc_ref)

    def lane_bcast(ref, h, lb, r):
        return jnp.broadcast_to(ref[h, lb, r:r + 1, :], (BF16_ROWS, LANES)).astype(BF16)

    def gate_tile(at_ref, ga_ref, c1_ref, p1_ref, row0, lb, rg, kq):
        lanes = slice(lb * LANES, (lb + 1) * LANES)
        keys = [slice(kq * GATE_KEYS + k * BF16_ROWS, kq * GATE_KEYS + (k + 1) * BF16_ROWS)
                for k in range(vregs)]
        key_rows = range(rg * GATE_ROWS, (rg + 1) * GATE_ROWS)
        g = {r: [None] * vregs for r in key_rows}
        for h in range(HEADS):
            r2 = [r2_scr[h, lb, ks, :] for ks in keys]
            e2 = [e2_scr[h, lb, ks, :] for ks in keys]
            for r in key_rows:
                c = lane_bcast(c1_ref, h, lb, row0 + r)
                p = lane_bcast(p1_ref, h, lb, row0 + r)
                for k in range(vregs):
                    term = jnp.where(r2[k] < c, e2[k], jnp.zeros_like(e2[k])) * p
                    g[r][k] = term if h == 0 else g[r][k] + term
        for r in key_rows:
            for k in range(vregs):
                rows = slice(r * N_KEYS + keys[k].start, r * N_KEYS + keys[k].stop)
                a = at_ref[rows, lanes].astype(BF16)
                ga_ref[rows, lanes] = g[r][k] * jax.nn.gelu(a)

    def half_step(half, at_new, at_cur, ga_new, ga_old, c1_ref, p1_ref, row0):
        u_rows = slice(half * HALF_EXPERTS // 2, (half + 1) * HALF_EXPERTS // 2)
        v_cols = slice(half * HALF_EXPERTS, (half + 1) * HALF_EXPERTS)
        def project(m, cols):
            rows = slice(m * HALF_EXPERTS // MXU_SPLIT, (m + 1) * HALF_EXPERTS // MXU_SPLIT)
            u_part = slice(u_rows.start + rows.start // 2, u_rows.start + rows.stop // 2)
            at_new[rows, cols] = _dot(pltpu.bitcast(u_ref[u_part, :], BF16),
                                      pltpu.bitcast(h2t_ref[:, cols], BF16))

        def accumulate(m, cols):
            rows = slice(m * D_MODEL // MXU_SPLIT, (m + 1) * D_MODEL // MXU_SPLIT)
            v_part = slice(rows.start // 2, rows.stop // 2)
            acc_ref[rows, cols] += _dot(pltpu.bitcast(vt_ref[v_part, v_cols], BF16), ga_old[:, cols])

        halves = [slice(n * PEER_TOKENS // 2, (n + 1) * PEER_TOKENS // 2) for n in range(2)]
        pieces = [(stage, m, cols) for stage in (project, accumulate) for cols in halves
                  for m in range(MXU_SPLIT)]
        tiles = [(lb, rg, kq) for lb in range(PEER_TOKENS // LANES)
                 for rg in range(rows_half // GATE_ROWS) for kq in range(N_KEYS // GATE_KEYS)]
        per_piece = len(tiles) // len(pieces)
        assert per_piece * len(pieces) == len(tiles)
        for n, (stage, m, cols) in enumerate(pieces):
            mine = tiles[n * per_piece:(n + 1) * per_piece]
            for lb, rg, kq in mine[:per_piece // 2]:
                gate_tile(at_cur, ga_new, c1_ref, p1_ref, row0, lb, rg, kq)
            stage(m, cols)
            for lb, rg, kq in mine[per_piece // 2:]:
                gate_tile(at_cur, ga_new, c1_ref, p1_ref, row0, lb, rg, kq)

    half_step(0, at0_ref, at1_ref, ga1_ref, ga0_ref, c1a_ref, p1a_ref, rows_half % SUBLANES)

    @pl.when(phase == 0)
    def _():
        load_keys()

    half_step(1, at1_ref, at0_ref, ga0_ref, ga1_ref, c1b_ref, p1b_ref, 0)

    @pl.when((phase == 0) & (j > 0))
    def _():
        x2 = x1_ref[...] + acc_ref[...].T
        h3 = _rms(x2, ln3_ref[...]).astype(BF16)
        gate = _sigmoid(_dot(h3, wpg_ref[...]))
        x3 = x2 + gate * _dot(pe_ref[...].astype(BF16), wple_ref[...])
        o_ref[...] = _rms(x3, lnf_ref[...]) if final else x3


def _peer(final, h2t, r2, e2, c1, p1, u_pk, vt_pk, x1, pe, ln3, wpg, wple, lnf):
    nt = T_ALL // PEER_TOKENS
    tiles_per_block = N_EXPERTS // HALF_EXPERTS
    n_tiles = nt * tiles_per_block
    assert tiles_per_block == 2 * PEER_STEPS_PER_BLOCK
    rows_half = HALF_EXPERTS // N_KEYS
    assert SUBLANES % rows_half == 0
    n_steps = n_tiles // 2 + 1
    lb_per_step = PEER_TOKENS // LANES

    prev_step = lambda j: jnp.maximum(j - 1, 0)
    tok_spec = lambda width: pl.BlockSpec(
        (PEER_TOKENS, width), lambda j: (prev_step(j) // PEER_STEPS_PER_BLOCK, 0))
    cur_block = lambda j: jnp.minimum(j // PEER_STEPS_PER_BLOCK, nt - 1)
    key_spec = pl.BlockSpec((HEADS, lb_per_step, N_KEYS, LANES), lambda j: (0, cur_block(j), 0, 0))

    def row_spec(tile_of_step):
        def index(j):
            s = jnp.clip(tile_of_step(j), 0, n_tiles - 1)
            return (0, s // tiles_per_block, (s % tiles_per_block) * rows_half // SUBLANES, 0, 0)
        return pl.BlockSpec((HEADS, lb_per_step, None, SUBLANES, LANES), index)

    rows_a = row_spec(lambda j: 2 * j - 1)
    rows_b = row_spec(lambda j: 2 * j)
    const = lambda shape: pl.BlockSpec(shape, lambda j: (0,) * len(shape), pipeline_mode=pl.Buffered(1))
    return pl.pallas_call(
        functools.partial(_peer_kernel, final),
        grid=(n_steps,),
        in_specs=[
            pl.BlockSpec((D_MODEL // 2, PEER_TOKENS), lambda j: (0, cur_block(j))),
            key_spec, key_spec, rows_a, rows_a, rows_b, rows_b,
            pl.BlockSpec((HALF_EXPERTS, D_MODEL), lambda j: (j % PEER_STEPS_PER_BLOCK, 0)),
            pl.BlockSpec((D_MODEL // 2, 2 * HALF_EXPERTS),
                         lambda j: (0, prev_step(j) % PEER_STEPS_PER_BLOCK)),
            tok_spec(D_MODEL), tok_spec(D_PLE),
            const((1, D_MODEL)), const((D_MODEL, D_MODEL)), const((D_PLE, D_MODEL)), const((1, D_MODEL)),
        ],
        out_specs=tok_spec(D_MODEL),
        out_shape=jax.ShapeDtypeStruct((T_ALL, D_MODEL), F32),
        scratch_shapes=[pltpu.VMEM((D_MODEL, PEER_TOKENS), F32)]
        + [pltpu.VMEM((HEADS, lb_per_step, N_KEYS, LANES), BF16)] * 2
        + [pltpu.VMEM((HALF_EXPERTS, PEER_TOKENS), F32)] * 2
        + [pltpu.VMEM((HALF_EXPERTS, PEER_TOKENS), BF16)] * 2,
        compiler_params=pltpu.CompilerParams(
            dimension_semantics=("arbitrary",), vmem_limit_bytes=VMEM_LIMIT),
        name="peer_dense",
    )(h2t, r2, e2, c1, p1, c1, p1, u_pk, vt_pk, x1, pe, ln3, wpg, wple, lnf)


def _pack_tables_kernel(u_ref, v_ref, upk_ref, vtpk_ref):
    upk_ref[...] = pltpu.bitcast(u_ref[...].astype(BF16), jnp.uint32)
    vtpk_ref[...] = pltpu.bitcast(v_ref[...].T.astype(BF16), jnp.uint32)


def _pack_tables(u_tab, v_tab):
    nb = N_EXPERTS // PACK_EXPERTS
    in_spec = pl.BlockSpec((None, PACK_EXPERTS, D_MODEL), lambda l, e: (l, e, 0))
    return pl.pallas_call(
        _pack_tables_kernel,
        grid=(DEPTH, nb),
        in_specs=[in_spec, in_spec],
        out_specs=[
            pl.BlockSpec((None, PACK_EXPERTS // 2, D_MODEL), lambda l, e: (l, e, 0)),
            pl.BlockSpec((None, D_MODEL // 2, PACK_EXPERTS), lambda l, e: (l, 0, e)),
        ],
        out_shape=[
            jax.ShapeDtypeStruct((DEPTH, N_EXPERTS // 2, D_MODEL), jnp.uint32),
            jax.ShapeDtypeStruct((DEPTH, D_MODEL // 2, N_EXPERTS), jnp.uint32),
        ],
        compiler_params=pltpu.CompilerParams(
            dimension_semantics=("arbitrary", "arbitrary"), vmem_limit_bytes=VMEM_LIMIT),
        name="pack_tables",
    )(u_tab, v_tab)


def kernel(x_prompt, x_sample, state_conv, p_prompt, p_sample, ln1, w_in, ln_v, w_s, b_s, conv_w,
           w_a, w_b, w_o, ln2, w_q, k1, k2, u_tab, v_tab, ln3, w_pg, w_ple, ln_f):
    assert x_prompt.shape == (N_PROMPT_SEQ, PROMPT_LEN, D_MODEL)
    assert x_sample.shape == (N_SAMPLE_SEQ, SAMPLE_LEN, D_MODEL)
    assert u_tab.shape == (DEPTH, N_EXPERTS, D_MODEL)

    x_p = x_prompt.reshape(T_PROMPT, D_MODEL)
    x_s = x_sample.reshape(T_SAMPLE, D_MODEL)
    sample_block = 0
    seqs_per_chunk = CHUNK // SAMPLE_LEN
    eye = jnp.eye(seqs_per_chunk, dtype=F32)
    lnf = ln_f.reshape(1, D_MODEL)
    u_pk, vt_pk = _pack_tables(u_tab, v_tab)
    conv_prompt, conv_sample, v_sample = [], [], []
    y = None

    for i in range(DEPTH):
        row = lambda a: a[i].reshape(1, -1)
        ws_p = w_s[i].astype(BF16)
        corner = w_s[i][:, :SAMPLE_LEN, :SAMPLE_LEN]
        ws_s = (eye[None, :, None, :, None] * corner[:, None, :, None, :]).reshape(
            N_GROUPS, CHUNK, CHUNK).astype(BF16)
        bias_p = jnp.repeat(b_s[i].T, GROUP, axis=1)
        bias_s = jnp.tile(jnp.repeat(b_s[i][:, :SAMPLE_LEN].T, GROUP, axis=1), (seqs_per_chunk, 1))
        shared = (conv_w[i], w_a[i].astype(BF16), w_b[i].astype(BF16), w_o[i].astype(BF16))
        front = (row(ln1), w_in[i].astype(BF16), row(ln_v))
        w_prompt = front + (ws_p, bias_p) + shared
        w_sample = front + (ws_s, bias_s) + shared

        prev = state_conv[i]
        zero = jnp.zeros((N_SAMPLE_SEQ, 1, D_B), F32)
        tap1 = jnp.concatenate([prev[:, 1:2], zero, zero, zero], axis=1).reshape(T_SAMPLE, D_B)
        tap2 = jnp.concatenate([prev[:, 0:1], prev[:, 1:2], zero, zero], axis=1).reshape(T_SAMPLE, D_B)

        x1_p, tail_p = _mixer_prompt(x_p, w_prompt)
        x1_s, v_s, cx_s = _mixer_sample(x_s, sample_block, w_sample, tap1, tap2)
        conv_prompt.append(tail_p)
        conv_sample.append(cx_s.reshape(N_SAMPLE_SEQ, SAMPLE_LEN, D_B)[:, SAMPLE_LEN - 2:])
        v_sample.append(v_s.reshape(N_SAMPLE_SEQ, SAMPLE_LEN, D_A))

        x1 = jnp.concatenate([x1_p, x1_s], axis=0)
        pe = jnp.concatenate([p_prompt[i].reshape(T_PROMPT, D_PLE),
                              p_sample[i].reshape(T_SAMPLE, D_PLE)], axis=0)

        h2t, r2, e2, c1, p1 = _select(x1, row(ln2), w_q[i].T.astype(BF16),
                                      k1[i].astype(BF16), k2[i].astype(BF16))
        final = i == DEPTH - 1
        r2 = r2.reshape(HEADS, T_ALL // LANES, N_KEYS, LANES)
        e2 = e2.reshape(HEADS, T_ALL // LANES, N_KEYS, LANES)
        y = _peer(final, h2t, r2, e2, c1, p1, u_pk[i], vt_pk[i],
                  x1, pe, row(ln3), w_pg[i].astype(BF16), w_ple[i].astype(BF16), lnf)
        x_p = y
        x_s = y
        sample_block = T_PROMPT // T_SAMPLE

    y_prompt = y[:T_PROMPT].reshape(N_PROMPT_SEQ, PROMPT_LEN, D_MODEL)
    y_sample = y[T_PROMPT:].reshape(N_SAMPLE_SEQ, SAMPLE_LEN, D_MODEL)
    return (y_prompt, y_sample, jnp.stack(conv_prompt), jnp.stack(conv_sample), jnp.stack(v_sample))
```

```python
import functools

import jax
import jax.numpy as jnp
from jax import lax
from jax.experimental import pallas as pl
from jax.experimental.pallas import tpu as pltpu

F32 = jnp.float32
BF16 = jnp.bfloat16

D_MODEL = 1024
D_A = 1024
D_B = 1024
D_PLE = 256
CHUNK = 128
N_GROUPS = 8
GROUP = D_A // N_GROUPS
N_KEYS = 128
N_EXPERTS = N_KEYS * N_KEYS
HEADS = 8
D_QUERY = 256
D_HALF = 128
TOPK = 16
EPS = 1e-6
DEPTH = 2

N_PROMPT_SEQ = 8
PROMPT_LEN = 2048
N_SAMPLE_SEQ = 128
SAMPLE_LEN = 4
T_PROMPT = N_PROMPT_SEQ * PROMPT_LEN
T_SAMPLE = N_SAMPLE_SEQ * SAMPLE_LEN
T_ALL = T_PROMPT + T_SAMPLE

LANES = 128
SUBLANES = 8
MIX_ROWS = 256
SEL_TOKENS = 512
PEER_TOKENS = 512
HALF_EXPERTS = 1024
PEER_STEPS_PER_BLOCK = N_EXPERTS // (2 * HALF_EXPERTS)
GATE_ROWS = 4
MXU_SPLIT = 1
PACK_EXPERTS = 1024
BF16_ROWS = 16
GATE_KEYS = 64
VMEM_LIMIT = 60 * 1024 * 1024
NEG = -1e30


def _rms(x, g):
    ms = jnp.mean(x * x, axis=-1, keepdims=True)
    return x * lax.rsqrt(ms + EPS) * g


def _sigmoid(x):
    return 1.0 / (1.0 + jnp.exp(-x))


def _dot(a, b):
    return jnp.dot(a, b, preferred_element_type=F32)


def _const_spec(shape, grid_rank):
    zeros = (0,) * len(shape)
    if grid_rank == 1:
        return pl.BlockSpec(shape, lambda i: zeros, pipeline_mode=pl.Buffered(1))
    return pl.BlockSpec(shape, lambda i, j: zeros, pipeline_mode=pl.Buffered(1))


def _mixer_front(x, ln1_ref, win_ref, lnv_ref, ws_ref, bias_ref, yin_ref, rows):
    h = _rms(x, ln1_ref[...]).astype(BF16)

    def proj(k):
        return _dot(h, win_ref[:, k * D_MODEL:(k + 1) * D_MODEL])

    u = jax.nn.gelu(proj(0))
    v = _rms(jax.nn.gelu(proj(1)), lnv_ref[...])
    vb = v.astype(BF16)
    r_i = lax.broadcasted_iota(jnp.int32, (CHUNK, CHUNK), 0)
    c_i = lax.broadcasted_iota(jnp.int32, (CHUNK, CHUNK), 1)
    causal = r_i >= c_i
    for g in range(N_GROUPS):
        w = jnp.where(causal, ws_ref[g], jnp.zeros((), BF16))
        cols = slice(g * GROUP, (g + 1) * GROUP)
        for c in range(rows // CHUNK):
            rws = slice(c * CHUNK, (c + 1) * CHUNK)
            s = _dot(w, vb[rws, cols]) + bias_ref[:, cols]
            yin_ref[rws, cols] = (u[rws, cols] * s).astype(BF16)
    return h, v, proj


def _mixer_back(x, proj, yc, wa_ref, wb_ref, wo_ref, yin_ref):
    ya = _dot(yin_ref[...], wa_ref[...])
    yb = _dot((proj(2) * yc).astype(BF16), wb_ref[...])
    mix = _sigmoid(proj(5)) * ya + _sigmoid(proj(6)) * yb
    return x + _dot(mix.astype(BF16), wo_ref[...])


def _mixer_prompt_kernel(x_ref, ln1_ref, win_ref, lnv_ref, ws_ref, bias_ref, cw_ref,
                         wa_ref, wb_ref, wo_ref, x1_ref, tail_ref, yin_ref, cbuf_ref):
    rows = MIX_ROWS
    x = x_ref[...]
    h, v, proj = _mixer_front(x, ln1_ref, win_ref, lnv_ref, ws_ref, bias_ref, yin_ref, rows)
    cx = proj(3) * proj(4)

    @pl.when(pl.program_id(1) == 0)
    def _():
        cbuf_ref[0:SUBLANES, :] = jnp.zeros((SUBLANES, D_B), F32)

    cbuf_ref[SUBLANES:SUBLANES + rows, :] = cx
    c1 = cbuf_ref[SUBLANES - 1:SUBLANES - 1 + rows, :]
    c2 = cbuf_ref[SUBLANES - 2:SUBLANES - 2 + rows, :]
    yc = cw_ref[0:1, :] * c2 + cw_ref[1:2, :] * c1 + cw_ref[2:3, :] * cx
    cbuf_ref[0:SUBLANES, :] = cx[rows - SUBLANES:rows, :]
    tail_ref[0] = cx[rows - 2:rows, :]
    x1_ref[...] = _mixer_back(x, proj, yc, wa_ref, wb_ref, wo_ref, yin_ref)


def _mixer_sample_kernel(x_ref, ln1_ref, win_ref, lnv_ref, ws_ref, bias_ref, cw_ref,
                         wa_ref, wb_ref, wo_ref, p1_ref, p2_ref,
                         x1_ref, v_ref, cx_ref, yin_ref, cbuf_ref):
    rows = T_SAMPLE
    x = x_ref[...]
    h, v, proj = _mixer_front(x, ln1_ref, win_ref, lnv_ref, ws_ref, bias_ref, yin_ref, rows)
    v_ref[...] = v
    cx = proj(3) * proj(4)
    cx_ref[...] = cx
    cbuf_ref[0:SUBLANES, :] = jnp.zeros((SUBLANES, D_B), F32)
    cbuf_ref[SUBLANES:SUBLANES + rows, :] = cx
    pos = lax.broadcasted_iota(jnp.int32, (rows, D_B), 0) & (SAMPLE_LEN - 1)
    c1 = jnp.where(pos >= 1, cbuf_ref[SUBLANES - 1:SUBLANES - 1 + rows, :], p1_ref[...])
    c2 = jnp.where(pos >= 2, cbuf_ref[SUBLANES - 2:SUBLANES - 2 + rows, :], p2_ref[...])
    yc = cw_ref[0:1, :] * c2 + cw_ref[1:2, :] * c1 + cw_ref[2:3, :] * cx
    x1_ref[...] = _mixer_back(x, proj, yc, wa_ref, wb_ref, wo_ref, yin_ref)


def _mixer_weight_specs(grid_rank):
    cs = functools.partial(_const_spec, grid_rank=grid_rank)
    return [
        cs((1, D_MODEL)),
        cs((D_MODEL, 7 * D_MODEL)),
        cs((1, D_A)),
        cs((N_GROUPS, CHUNK, CHUNK)),
        cs((CHUNK, D_A)),
        cs((3, D_B)),
        cs((D_A, D_MODEL)),
        cs((D_B, D_MODEL)),
        cs((D_MODEL, D_MODEL)),
    ]


def _mixer_prompt(x2d, weights):
    nj = PROMPT_LEN // MIX_ROWS
    return pl.pallas_call(
        _mixer_prompt_kernel,
        grid=(N_PROMPT_SEQ, nj),
        in_specs=[pl.BlockSpec((MIX_ROWS, D_MODEL), lambda b, j: (b * nj + j, 0))]
        + _mixer_weight_specs(2),
        out_specs=[
            pl.BlockSpec((MIX_ROWS, D_MODEL), lambda b, j: (b * nj + j, 0)),
            pl.BlockSpec((1, 2, D_B), lambda b, j: (b, 0, 0)),
        ],
        out_shape=[
            jax.ShapeDtypeStruct((T_PROMPT, D_MODEL), F32),
            jax.ShapeDtypeStruct((N_PROMPT_SEQ, 2, D_B), F32),
        ],
        scratch_shapes=[
            pltpu.VMEM((MIX_ROWS, D_A), BF16),
            pltpu.VMEM((MIX_ROWS + SUBLANES, D_B), F32),
        ],
        compiler_params=pltpu.CompilerParams(
            dimension_semantics=("arbitrary", "arbitrary"), vmem_limit_bytes=VMEM_LIMIT),
        name="mixer_prompt",
    )(x2d, *weights)


def _mixer_sample(x2d, row_block, weights, p1, p2):
    full = lambda shape: pl.BlockSpec(shape, lambda i: (0,) * len(shape))
    return pl.pallas_call(
        _mixer_sample_kernel,
        grid=(1,),
        in_specs=[pl.BlockSpec((T_SAMPLE, D_MODEL), lambda i: (row_block, 0))]
        + _mixer_weight_specs(1)
        + [full((T_SAMPLE, D_B)), full((T_SAMPLE, D_B))],
        out_specs=[full((T_SAMPLE, D_MODEL)), full((T_SAMPLE, D_A)), full((T_SAMPLE, D_B))],
        out_shape=[
            jax.ShapeDtypeStruct((T_SAMPLE, D_MODEL), F32),
            jax.ShapeDtypeStruct((T_SAMPLE, D_A), F32),
            jax.ShapeDtypeStruct((T_SAMPLE, D_B), F32),
        ],
        scratch_shapes=[
            pltpu.VMEM((T_SAMPLE, D_A), BF16),
            pltpu.VMEM((T_SAMPLE + SUBLANES, D_B), F32),
        ],
        compiler_params=pltpu.CompilerParams(
            dimension_semantics=("arbitrary",), vmem_limit_bytes=VMEM_LIMIT),
        name="mixer_sample",
    )(x2d, *weights, p1, p2)


def _cmpx(v, i, j):
    hi = jnp.maximum(v[i], v[j])
    lo = jnp.minimum(v[i], v[j])
    v[i] = hi
    v[j] = lo


def _bitonic_merge16(v):
    j = TOPK // 2
    while j >= 1:
        for i in range(TOPK):
            l = i ^ j
            if l > i:
                _cmpx(v, i, l)
        j //= 2


def _sort16(v):
    k = 2
    while k <= TOPK:
        j = k // 2
        while j >= 1:
            for i in range(TOPK):
                l = i ^ j
                if l > i:
                    if (i & k) == 0:
                        _cmpx(v, i, l)
                    else:
                        _cmpx(v, l, i)
            j //= 2
        k *= 2


def _top16(s):
    v = [s[k] for k in range(TOPK)]
    _sort16(v)
    for shift in (4, 2, 1):
        w = [pltpu.roll(v[k], shift, 0) for k in range(TOPK)]
        v = [jnp.maximum(v[k], w[TOPK - 1 - k]) for k in range(TOPK)]
        _bitonic_merge16(v)
    return v


def _sublane_allreduce(x, op):
    for shift in (4, 2, 1):
        x = op(x, pltpu.roll(x, shift, 0))
    return x


def _on_sublanes(vals, sub):
    out = vals[SUBLANES - 1]
    for r in range(SUBLANES - 2, -1, -1):
        out = jnp.where(sub == r, vals[r], out)
    return out


def _count_above(t, x, above):
    bits = []
    for step in range(4):
        width = TOPK >> (step + 1)
        pivots = [t[base + width - 1] for base in range(0, TOPK, 2 * width)]
        for bit in reversed(bits):
            pivots = [jnp.where(bit, hi, lo) for lo, hi in zip(pivots[0::2], pivots[1::2])]
        pivot = pivots[0]
        bits.append(above(pivot[None] if pivot.ndim == 2 else pivot, x))
    count = jnp.where(above(t[TOPK - 1][None], x), 1.0, 0.0)
    for step, bit in enumerate(bits):
        count = count + jnp.where(bit, float(TOPK >> (step + 1)), 0.0)
    return count


def _select_kernel(x1p_ref, x1s_ref, ln2_ref, wqt_ref, k1_ref, k2_ref,
                   h2t_ref, r2_ref, e2_ref, c1_ref, p1_ref, s1_scr, s2_scr):
    tokens = SEL_TOKENS
    is_sample = pl.program_id(0) == T_PROMPT // SEL_TOKENS
    h2 = _rms(jnp.where(is_sample, x1s_ref[...], x1p_ref[...]), ln2_ref[...])
    h2t = h2.T.astype(BF16)
    h2t_ref[...] = pltpu.bitcast(h2t, jnp.uint32)
    qt = _dot(wqt_ref[...], h2t)
    for h in range(HEADS):
        q1 = qt[h * D_QUERY:h * D_QUERY + D_HALF].astype(BF16)
        q2 = qt[h * D_QUERY + D_HALF:(h + 1) * D_QUERY].astype(BF16)
        s1_scr[h] = _dot(k1_ref[...], q1).reshape(TOPK, SUBLANES, tokens)
        s2_scr[h] = _dot(k2_ref[...], q2).reshape(TOPK, SUBLANES, tokens)

    sub = lax.broadcasted_iota(jnp.int32, (SUBLANES, LANES), 0)

    def lane_block(lb, carry):
        lanes = pl.ds(pl.multiple_of(lb * LANES, LANES), LANES)
        for h in range(HEADS):
            s1 = s1_scr[h, :, :, lanes]
            s2 = s2_scr[h, :, :, lanes]
            t1 = _top16(s1)
            t2 = _top16(s2)
            t2lo = _on_sublanes(t2[0:8], sub)
            t2hi = _on_sublanes(t2[8:16], sub)
            t1hi = _on_sublanes(t1[8:16], sub)
            cands = [t1[0] + t2lo, t1[0] + t2hi, t1[1] + t2lo]
            for a, nb in ((2, 5), (3, 4), (4, 3), (5, 2), (6, 2), (7, 2)):
                cands.append(jnp.where(sub < nb, t1[a] + t2lo, NEG))
            cands.append(t1hi + t2[0])
            cur = list(cands)
            tops = []
            for it in range(TOPK + 1):
                m = functools.reduce(jnp.maximum, cur)
                m = _sublane_allreduce(m, jnp.maximum)
                tops.append(m)
                if it < TOPK:
                    cur = [jnp.where(c == m, NEG, c) for c in cur]
            tau = 0.5 * (tops[TOPK - 1] + tops[TOPK])
            z = functools.reduce(
                jnp.add, [jnp.where(c >= tau, jnp.exp(c - tops[0]), 0.0) for c in cands])
            z = _sublane_allreduce(z, jnp.add)
            inv_z = 1.0 / z
            p1 = jnp.where(s1 >= t1[TOPK - 1][None], jnp.exp(s1 - t1[0][None]) * inv_z[None], 0.0)
            e2 = jnp.where(s2 >= t2[TOPK - 1][None], jnp.exp(s2 - t2[0][None]), 0.0)
            th = tau[None] - s1
            r2_ref[h, lb] = _count_above(t2, s2, jnp.greater)
            e2_ref[h, lb] = e2
            c1_ref[h, lb] = _count_above(t2, th, jnp.greater_equal)
            p1_ref[h, lb] = p1
        return carry

    lax.fori_loop(0, tokens // LANES, lane_block, 0)


def _select(x1_p, x1_s, ln2, wqt, k1, k2):
    nb = T_ALL // SEL_TOKENS
    n_prompt = T_PROMPT // SEL_TOKENS
    assert T_SAMPLE == SEL_TOKENS and nb == n_prompt + 1
    lb_per_step = SEL_TOKENS // LANES
    n_lb = T_ALL // LANES
    key_shape = jax.ShapeDtypeStruct((HEADS, n_lb, TOPK, SUBLANES, LANES), F32)
    key_spec = pl.BlockSpec((HEADS, lb_per_step, TOPK, SUBLANES, LANES), lambda i: (0, i, 0, 0, 0))
    scr = pltpu.VMEM((HEADS, TOPK, SUBLANES, SEL_TOKENS), F32)
    return pl.pallas_call(
        _select_kernel,
        grid=(nb,),
        in_specs=[
            pl.BlockSpec((SEL_TOKENS, D_MODEL), lambda i: (jnp.minimum(i, n_prompt - 1), 0)),
            pl.BlockSpec((SEL_TOKENS, D_MODEL), lambda i: (0, 0)),
            _const_spec((1, D_MODEL), 1),
            _const_spec((HEADS * D_QUERY, D_MODEL), 1),
            _const_spec((N_KEYS, D_HALF), 1),
            _const_spec((N_KEYS, D_HALF), 1),
        ],
        out_specs=[pl.BlockSpec((D_MODEL // 2, SEL_TOKENS), lambda i: (0, i))] + [key_spec] * 4,
        out_shape=[jax.ShapeDtypeStruct((D_MODEL // 2, T_ALL), jnp.uint32)] + [key_shape] * 4,
        scratch_shapes=[scr, scr],
        compiler_params=pltpu.CompilerParams(
            dimension_semantics=("arbitrary",), vmem_limit_bytes=VMEM_LIMIT),
        name="peer_select",
    )(x1_p, x1_s, ln2, wqt, k1, k2)


def _peer_kernel(final, h2t_ref, r2_ref, e2_ref, c1a_ref, p1a_ref, c1b_ref, p1b_ref, u_ref, vt_ref,
                 x1p_ref, x1s_ref, pep_ref, pes_ref, ln3_ref, wpg_ref, wple_ref, lnf_ref, *out_and_scratch):
    n_out = 2 if final else 1
    o_refs = out_and_scratch[:n_out]
    acc_ref, r2_scr, e2_scr, at0_ref, at1_ref, ga0_ref, ga1_ref = out_and_scratch[n_out:]
    _peer_body(final, h2t_ref, r2_ref, e2_ref, c1a_ref, p1a_ref, c1b_ref, p1b_ref, u_ref, vt_ref,
               x1p_ref, x1s_ref, pep_ref, pes_ref, ln3_ref, wpg_ref, wple_ref, lnf_ref, o_refs,
               acc_ref, r2_scr, e2_scr, at0_ref, at1_ref, ga0_ref, ga1_ref)


def _peer_body(final, h2t_ref, r2_ref, e2_ref, c1a_ref, p1a_ref, c1b_ref, p1b_ref, u_ref, vt_ref,
               x1p_ref, x1s_ref, pep_ref, pes_ref, ln3_ref, wpg_ref, wple_ref, lnf_ref, o_refs,
               acc_ref, r2_scr, e2_scr, at0_ref, at1_ref, ga0_ref, ga1_ref):
    j = pl.program_id(0)
    phase = j % PEER_STEPS_PER_BLOCK
    rows_half = HALF_EXPERTS // N_KEYS
    vregs = GATE_KEYS // BF16_ROWS

    def load_keys():
        for h in range(HEADS):
            r2_scr[h] = r2_ref[h].astype(BF16)
            e2_scr[h] = e2_ref[h].astype(BF16)

    @pl.when(j == 0)
    def _():
        at1_ref[...] = jnp.zeros_like(at1_ref)
        ga0_ref[...] = jnp.zeros_like(ga0_ref)
        ga1_ref[...] = jnp.zeros_like(ga1_ref)
        load_keys()

    @pl.when((phase == 1) | (j == 0))
    def _():
        acc_ref[...] = jnp.zeros_like(acc_ref)

    def lane_bcast(ref, h, lb, r):
        return jnp.broadcast_to(ref[h, lb, r:r + 1, :], (BF16_ROWS, LANES)).astype(BF16)

    def gate_tile(at_ref, ga_ref, c1_ref, p1_ref, row0, lb, rg, kq):
        lanes = slice(lb * LANES, (lb + 1) * LANES)
        keys = [slice(kq * GATE_KEYS + k * BF16_ROWS, kq * GATE_KEYS + (k + 1) * BF16_ROWS)
                for k in range(vregs)]
        key_rows = range(rg * GATE_ROWS, (rg + 1) * GATE_ROWS)
        g = {r: [None] * vregs for r in key_rows}
        for h in range(HEADS):
            r2 = [r2_scr[h, lb, ks, :] for ks in keys]
            e2 = [e2_scr[h, lb, ks, :] for ks in keys]
            for r in key_rows:
                c = lane_bcast(c1_ref, h, lb, row0 + r)
                p = lane_bcast(p1_ref, h, lb, row0 + r)
                for k in range(vregs):
                    term = jnp.where(r2[k] < c, e2[k], jnp.zeros_like(e2[k])) * p
                    g[r][k] = term if h == 0 else g[r][k] + term
        for r in key_rows:
            for k in range(vregs):
                rows = slice(r * N_KEYS + keys[k].start, r * N_KEYS + keys[k].stop)
                a = at_ref[rows, lanes].astype(BF16)
                ga_ref[rows, lanes] = g[r][k] * jax.nn.gelu(a)

    def half_step(half, at_new, at_cur, ga_new, ga_old, c1_ref, p1_ref, row0):
        u_rows = slice(half * HALF_EXPERTS // 2, (half + 1) * HALF_EXPERTS // 2)
        v_cols = slice(half * HALF_EXPERTS, (half + 1) * HALF_EXPERTS)
        def project(m, cols):
            rows = slice(m * HALF_EXPERTS // MXU_SPLIT, (m + 1) * HALF_EXPERTS // MXU_SPLIT)
            u_part = slice(u_rows.start + rows.start // 2, u_rows.start + rows.stop // 2)
            at_new[rows, cols] = _dot(pltpu.bitcast(u_ref[u_part, :], BF16),
                                      pltpu.bitcast(h2t_ref[:, cols], BF16))

        def accumulate(m, cols):
            rows = slice(m * D_MODEL // MXU_SPLIT, (m + 1) * D_MODEL // MXU_SPLIT)
            v_part = slice(rows.start // 2, rows.stop // 2)
            acc_ref[rows, cols] += _dot(pltpu.bitcast(vt_ref[v_part, v_cols], BF16), ga_old[:, cols])

        halves = [slice(n * PEER_TOKENS // 2, (n + 1) * PEER_TOKENS // 2) for n in range(2)]
        pieces = [(stage, m, cols) for stage in (project, accumulate) for cols in halves
                  for m in range(MXU_SPLIT)]
        tiles = [(lb, rg, kq) for lb in range(PEER_TOKENS // LANES)
                 for rg in range(rows_half // GATE_ROWS) for kq in range(N_KEYS // GATE_KEYS)]
        per_piece = len(tiles) // len(pieces)
        assert per_piece * len(pieces) == len(tiles)
        for n, (stage, m, cols) in enumerate(pieces):
            mine = tiles[n * per_piece:(n + 1) * per_piece]
            for lb, rg, kq in mine[:per_piece // 2]:
                gate_tile(at_cur, ga_new, c1_ref, p1_ref, row0, lb, rg, kq)
            stage(m, cols)
            for lb, rg, kq in mine[per_piece // 2:]:
                gate_tile(at_cur, ga_new, c1_ref, p1_ref, row0, lb, rg, kq)

    half_step(0, at0_ref, at1_ref, ga1_ref, ga0_ref, c1a_ref, p1a_ref, rows_half % SUBLANES)

    @pl.when(phase == 0)
    def _():
        load_keys()

    half_step(1, at1_ref, at0_ref, ga0_ref, ga1_ref, c1b_ref, p1b_ref, 0)

    @pl.when((phase == 0) & (j > 0))
    def _():
        is_sample = (j - 1) // PEER_STEPS_PER_BLOCK == T_PROMPT // PEER_TOKENS
        pe = jnp.where(is_sample, pes_ref[...], pep_ref[...])
        x2 = jnp.where(is_sample, x1s_ref[...], x1p_ref[...]) + acc_ref[...].T
        h3 = _rms(x2, ln3_ref[...]).astype(BF16)
        gate = _sigmoid(_dot(h3, wpg_ref[...]))
        x3 = x2 + gate * _dot(pe.astype(BF16), wple_ref[...])
        if final:
            y = _rms(x3, lnf_ref[...])

            @pl.when(jnp.logical_not(is_sample))
            def _():
                o_refs[0][...] = y

            @pl.when(is_sample)
            def _():
                o_refs[1][...] = y
        else:
            o_refs[0][...] = x3


def _peer(final, layer, h2t, r2, e2, c1, p1, u_pk, vt_pk, x1_p, x1_s, pe_p, pe_s, ln3, wpg, wple, lnf):
    nt = T_ALL // PEER_TOKENS
    n_prompt = T_PROMPT // PEER_TOKENS
    assert T_SAMPLE == PEER_TOKENS and nt == n_prompt + 1
    tiles_per_block = N_EXPERTS // HALF_EXPERTS
    n_tiles = nt * tiles_per_block
    assert tiles_per_block == 2 * PEER_STEPS_PER_BLOCK
    rows_half = HALF_EXPERTS // N_KEYS
    assert SUBLANES % rows_half == 0
    n_steps = n_tiles // 2 + 1
    lb_per_step = PEER_TOKENS // LANES

    prev_step = lambda j: jnp.maximum(j - 1, 0)
    tok_spec = lambda width: pl.BlockSpec(
        (PEER_TOKENS, width), lambda j: (prev_step(j) // PEER_STEPS_PER_BLOCK, 0))
    cur_block = lambda j: jnp.minimum(j // PEER_STEPS_PER_BLOCK, nt - 1)
    key_spec = pl.BlockSpec((HEADS, lb_per_step, N_KEYS, LANES), lambda j: (0, cur_block(j), 0, 0))

    def row_spec(tile_of_step):
        def index(j):
            s = jnp.clip(tile_of_step(j), 0, n_tiles - 1)
            return (0, s // tiles_per_block, (s % tiles_per_block) * rows_half // SUBLANES, 0, 0)
        return pl.BlockSpec((HEADS, lb_per_step, None, SUBLANES, LANES), index)

    rows_a = row_spec(lambda j: 2 * j - 1)
    rows_b = row_spec(lambda j: 2 * j)
    const = lambda shape: pl.BlockSpec(shape, lambda j: (0,) * len(shape), pipeline_mode=pl.Buffered(1))
    prompt_block = lambda j: jnp.minimum(prev_step(j) // PEER_STEPS_PER_BLOCK, n_prompt - 1)
    if final:
        out_specs = [pl.BlockSpec((PEER_TOKENS, D_MODEL), lambda j: (prompt_block(j), 0)),
                     pl.BlockSpec((PEER_TOKENS, D_MODEL), lambda j: (0, 0))]
        out_shape = [jax.ShapeDtypeStruct((T_PROMPT, D_MODEL), F32),
                     jax.ShapeDtypeStruct((T_SAMPLE, D_MODEL), F32)]
    else:
        out_specs = [tok_spec(D_MODEL)]
        out_shape = [jax.ShapeDtypeStruct((T_ALL, D_MODEL), F32)]
    return pl.pallas_call(
        functools.partial(_peer_kernel, final),
        grid=(n_steps,),
        in_specs=[
            pl.BlockSpec((D_MODEL // 2, PEER_TOKENS), lambda j: (0, cur_block(j))),
            key_spec, key_spec, rows_a, rows_a, rows_b, rows_b,
            pl.BlockSpec((None, HALF_EXPERTS, D_MODEL), lambda j: (layer, j % PEER_STEPS_PER_BLOCK, 0)),
            pl.BlockSpec((None, D_MODEL // 2, 2 * HALF_EXPERTS),
                         lambda j: (layer, 0, prev_step(j) % PEER_STEPS_PER_BLOCK)),
            pl.BlockSpec((PEER_TOKENS, D_MODEL), lambda j: (prompt_block(j), 0)),
            pl.BlockSpec((PEER_TOKENS, D_MODEL), lambda j: (0, 0), pipeline_mode=pl.Buffered(1)),
            pl.BlockSpec((None, PEER_TOKENS, D_PLE), lambda j: (layer, prompt_block(j), 0)),
            pl.BlockSpec((None, PEER_TOKENS, D_PLE), lambda j: (layer, 0, 0),
                         pipeline_mode=pl.Buffered(1)),
            const((1, D_MODEL)), const((D_MODEL, D_MODEL)), const((D_PLE, D_MODEL)), const((1, D_MODEL)),
        ],
        out_specs=out_specs,
        out_shape=out_shape,
        scratch_shapes=[pltpu.VMEM((D_MODEL, PEER_TOKENS), F32)]
        + [pltpu.VMEM((HEADS, lb_per_step, N_KEYS, LANES), BF16)] * 2
        + [pltpu.VMEM((HALF_EXPERTS, PEER_TOKENS), F32)] * 2
        + [pltpu.VMEM((HALF_EXPERTS, PEER_TOKENS), BF16)] * 2,
        compiler_params=pltpu.CompilerParams(
            dimension_semantics=("arbitrary",), vmem_limit_bytes=VMEM_LIMIT),
        name="peer_dense",
    )(h2t, r2, e2, c1, p1, c1, p1, u_pk, vt_pk, x1_p, x1_s, pe_p, pe_s, ln3, wpg, wple, lnf)


def _pack_tables_kernel(u_ref, v_ref, upk_ref, vtpk_ref):
    upk_ref[...] = pltpu.bitcast(u_ref[...].astype(BF16), jnp.uint32)
    vtpk_ref[...] = pltpu.bitcast(v_ref[...].T.astype(BF16), jnp.uint32)


def _pack_tables(u_tab, v_tab):
    nb = N_EXPERTS // PACK_EXPERTS
    in_spec = pl.BlockSpec((None, PACK_EXPERTS, D_MODEL), lambda l, e: (l, e, 0))
    return pl.pallas_call(
        _pack_tables_kernel,
        grid=(DEPTH, nb),
        in_specs=[in_spec, in_spec],
        out_specs=[
            pl.BlockSpec((None, PACK_EXPERTS // 2, D_MODEL), lambda l, e: (l, e, 0)),
            pl.BlockSpec((None, D_MODEL // 2, PACK_EXPERTS), lambda l, e: (l, 0, e)),
        ],
        out_shape=[
            jax.ShapeDtypeStruct((DEPTH, N_EXPERTS // 2, D_MODEL), jnp.uint32),
            jax.ShapeDtypeStruct((DEPTH, D_MODEL // 2, N_EXPERTS), jnp.uint32),
        ],
        compiler_params=pltpu.CompilerParams(
            dimension_semantics=("arbitrary", "arbitrary"), vmem_limit_bytes=VMEM_LIMIT),
        name="pack_tables",
    )(u_tab, v_tab)


def kernel(x_prompt, x_sample, state_conv, p_prompt, p_sample, ln1, w_in, ln_v, w_s, b_s, conv_w,
           w_a, w_b, w_o, ln2, w_q, k1, k2, u_tab, v_tab, ln3, w_pg, w_ple, ln_f):
    assert x_prompt.shape == (N_PROMPT_SEQ, PROMPT_LEN, D_MODEL)
    assert x_sample.shape == (N_SAMPLE_SEQ, SAMPLE_LEN, D_MODEL)
    assert u_tab.shape == (DEPTH, N_EXPERTS, D_MODEL)

    x_p = x_prompt.reshape(T_PROMPT, D_MODEL)
    x_s = x_sample.reshape(T_SAMPLE, D_MODEL)
    sample_block = 0
    seqs_per_chunk = CHUNK // SAMPLE_LEN
    eye = jnp.eye(seqs_per_chunk, dtype=F32)
    lnf = ln_f.reshape(1, D_MODEL)
    u_pk, vt_pk = _pack_tables(u_tab, v_tab)
    pe_p = p_prompt.reshape(DEPTH, T_PROMPT, D_PLE)
    pe_s = p_sample.reshape(DEPTH, T_SAMPLE, D_PLE)
    conv_prompt, conv_sample, v_sample = [], [], []
    y = None

    for i in range(DEPTH):
        row = lambda a: a[i].reshape(1, -1)
        ws_p = w_s[i].astype(BF16)
        corner = w_s[i][:, :SAMPLE_LEN, :SAMPLE_LEN]
        ws_s = (eye[None, :, None, :, None] * corner[:, None, :, None, :]).reshape(
            N_GROUPS, CHUNK, CHUNK).astype(BF16)
        bias_p = jnp.repeat(b_s[i].T, GROUP, axis=1)
        bias_s = jnp.tile(jnp.repeat(b_s[i][:, :SAMPLE_LEN].T, GROUP, axis=1), (seqs_per_chunk, 1))
        shared = (conv_w[i], w_a[i].astype(BF16), w_b[i].astype(BF16), w_o[i].astype(BF16))
        front = (row(ln1), w_in[i].astype(BF16), row(ln_v))
        w_prompt = front + (ws_p, bias_p) + shared
        w_sample = front + (ws_s, bias_s) + shared

        prev = state_conv[i]
        zero = jnp.zeros((N_SAMPLE_SEQ, 1, D_B), F32)
        tap1 = jnp.concatenate([prev[:, 1:2], zero, zero, zero], axis=1).reshape(T_SAMPLE, D_B)
        tap2 = jnp.concatenate([prev[:, 0:1], prev[:, 1:2], zero, zero], axis=1).reshape(T_SAMPLE, D_B)

        x1_p, tail_p = _mixer_prompt(x_p, w_prompt)
        x1_s, v_s, cx_s = _mixer_sample(x_s, sample_block, w_sample, tap1, tap2)
        conv_prompt.append(tail_p)
        conv_sample.append(cx_s.reshape(N_SAMPLE_SEQ, SAMPLE_LEN, D_B)[:, SAMPLE_LEN - 2:])
        v_sample.append(v_s.reshape(N_SAMPLE_SEQ, SAMPLE_LEN, D_A))

        h2t, r2, e2, c1, p1 = _select(x1_p, x1_s, row(ln2), w_q[i].T.astype(BF16),
                                      k1[i].astype(BF16), k2[i].astype(BF16))
        final = i == DEPTH - 1
        r2 = r2.reshape(HEADS, T_ALL // LANES, N_KEYS, LANES)
        e2 = e2.reshape(HEADS, T_ALL // LANES, N_KEYS, LANES)
        y = _peer(final, i, h2t, r2, e2, c1, p1, u_pk, vt_pk,
                  x1_p, x1_s, pe_p, pe_s, row(ln3), w_pg[i].astype(BF16), w_ple[i].astype(BF16), lnf)
        x_p = y[0]
        x_s = y[0]
        sample_block = T_PROMPT // T_SAMPLE

    y_prompt = y[0].reshape(N_PROMPT_SEQ, PROMPT_LEN, D_MODEL)
    y_sample = y[1].reshape(N_SAMPLE_SEQ, SAMPLE_LEN, D_MODEL)
    return (y_prompt, y_sample, jnp.stack(conv_prompt), jnp.stack(conv_sample), jnp.stack(v_sample))
```

```python
import functools

import jax
import jax.numpy as jnp
from jax import lax
from jax.experimental import pallas as pl
from jax.experimental.pallas import tpu as pltpu

F32 = jnp.float32
BF16 = jnp.bfloat16

D_MODEL = 1024
D_A = 1024
D_B = 1024
D_PLE = 256
CHUNK = 128
N_GROUPS = 8
GROUP = D_A // N_GROUPS
N_KEYS = 128
N_EXPERTS = N_KEYS * N_KEYS
HEADS = 8
D_QUERY = 256
D_HALF = 128
TOPK = 16
EPS = 1e-6
DEPTH = 2

N_PROMPT_SEQ = 8
PROMPT_LEN = 2048
N_SAMPLE_SEQ = 128
SAMPLE_LEN = 4
T_PROMPT = N_PROMPT_SEQ * PROMPT_LEN
T_SAMPLE = N_SAMPLE_SEQ * SAMPLE_LEN
T_ALL = T_PROMPT + T_SAMPLE

LANES = 128
SUBLANES = 8
MIX_ROWS = 512
SEL_TOKENS = 512
PEER_TOKENS = 512
HALF_EXPERTS = 1024
PEER_STEPS_PER_BLOCK = N_EXPERTS // (2 * HALF_EXPERTS)
GATE_ROWS = 4
MXU_SPLIT = 1
TILES_AROUND_PIECE = (4, 4, 4, 4)
PACK_EXPERTS = 1024
BF16_ROWS = 16
GATE_KEYS = 64
VMEM_LIMIT = 60 * 1024 * 1024
NEG = -1e30


def _rms(x, g):
    ms = jnp.mean(x * x, axis=-1, keepdims=True)
    return x * lax.rsqrt(ms + EPS) * g


def _sigmoid(x):
    return 1.0 / (1.0 + jnp.exp(-x))


def _dot(a, b):
    return jnp.dot(a, b, preferred_element_type=F32)


def _const_spec(shape, grid_rank):
    zeros = (0,) * len(shape)
    if grid_rank == 1:
        return pl.BlockSpec(shape, lambda i: zeros, pipeline_mode=pl.Buffered(1))
    return pl.BlockSpec(shape, lambda i, j: zeros, pipeline_mode=pl.Buffered(1))


def _mixer_front(x, ln1_ref, win_ref, lnv_ref, ws_ref, bias_ref, yin_ref, rows):
    h = _rms(x, ln1_ref[...]).astype(BF16)

    def proj(k):
        return _dot(h, win_ref[:, k * D_MODEL:(k + 1) * D_MODEL])

    u = jax.nn.gelu(proj(0))
    v = _rms(jax.nn.gelu(proj(1)), lnv_ref[...])
    vb = v.astype(BF16)
    r_i = lax.broadcasted_iota(jnp.int32, (CHUNK, CHUNK), 0)
    c_i = lax.broadcasted_iota(jnp.int32, (CHUNK, CHUNK), 1)
    causal = r_i >= c_i
    for g in range(N_GROUPS):
        w = jnp.where(causal, ws_ref[g], jnp.zeros((), BF16))
        cols = slice(g * GROUP, (g + 1) * GROUP)
        for c in range(rows // CHUNK):
            rws = slice(c * CHUNK, (c + 1) * CHUNK)
            s = _dot(w, vb[rws, cols]) + bias_ref[:, cols]
            yin_ref[rws, cols] = (u[rws, cols] * s).astype(BF16)
    return h, v, proj


def _mixer_back(x, proj, yc, wa_ref, wb_ref, wo_ref, yin_ref):
    ya = _dot(yin_ref[...], wa_ref[...])
    yb = _dot((proj(2) * yc).astype(BF16), wb_ref[...])
    mix = _sigmoid(proj(5)) * ya + _sigmoid(proj(6)) * yb
    return x + _dot(mix.astype(BF16), wo_ref[...])


def _mixer_prompt_kernel(x_ref, ln1_ref, win_ref, lnv_ref, ws_ref, bias_ref, cw_ref,
                         wa_ref, wb_ref, wo_ref, x1_ref, tail_ref, yin_ref, cbuf_ref):
    rows = MIX_ROWS
    x = x_ref[...]
    h, v, proj = _mixer_front(x, ln1_ref, win_ref, lnv_ref, ws_ref, bias_ref, yin_ref, rows)
    cx = proj(3) * proj(4)

    @pl.when(pl.program_id(1) == 0)
    def _():
        cbuf_ref[0:SUBLANES, :] = jnp.zeros((SUBLANES, D_B), F32)

    cbuf_ref[SUBLANES:SUBLANES + rows, :] = cx
    c1 = cbuf_ref[SUBLANES - 1:SUBLANES - 1 + rows, :]
    c2 = cbuf_ref[SUBLANES - 2:SUBLANES - 2 + rows, :]
    yc = cw_ref[0:1, :] * c2 + cw_ref[1:2, :] * c1 + cw_ref[2:3, :] * cx
    cbuf_ref[0:SUBLANES, :] = cx[rows - SUBLANES:rows, :]
    tail_ref[0] = cx[rows - 2:rows, :]
    x1_ref[...] = _mixer_back(x, proj, yc, wa_ref, wb_ref, wo_ref, yin_ref)


def _mixer_sample_kernel(x_ref, ln1_ref, win_ref, lnv_ref, ws_ref, bias_ref, cw_ref,
                         wa_ref, wb_ref, wo_ref, p1_ref, p2_ref,
                         x1_ref, v_ref, cx_ref, yin_ref, cbuf_ref):
    rows = T_SAMPLE
    x = x_ref[...]
    h, v, proj = _mixer_front(x, ln1_ref, win_ref, lnv_ref, ws_ref, bias_ref, yin_ref, rows)
    v_ref[...] = v
    cx = proj(3) * proj(4)
    cx_ref[...] = cx
    cbuf_ref[0:SUBLANES, :] = jnp.zeros((SUBLANES, D_B), F32)
    cbuf_ref[SUBLANES:SUBLANES + rows, :] = cx
    pos = lax.broadcasted_iota(jnp.int32, (rows, D_B), 0) & (SAMPLE_LEN - 1)
    c1 = jnp.where(pos >= 1, cbuf_ref[SUBLANES - 1:SUBLANES - 1 + rows, :], p1_ref[...])
    c2 = jnp.where(pos >= 2, cbuf_ref[SUBLANES - 2:SUBLANES - 2 + rows, :], p2_ref[...])
    yc = cw_ref[0:1, :] * c2 + cw_ref[1:2, :] * c1 + cw_ref[2:3, :] * cx
    x1_ref[...] = _mixer_back(x, proj, yc, wa_ref, wb_ref, wo_ref, yin_ref)


def _mixer_weight_specs(grid_rank):
    cs = functools.partial(_const_spec, grid_rank=grid_rank)
    return [
        cs((1, D_MODEL)),
        cs((D_MODEL, 7 * D_MODEL)),
        cs((1, D_A)),
        cs((N_GROUPS, CHUNK, CHUNK)),
        cs((CHUNK, D_A)),
        cs((3, D_B)),
        cs((D_A, D_MODEL)),
        cs((D_B, D_MODEL)),
        cs((D_MODEL, D_MODEL)),
    ]


def _mixer_prompt(x2d, weights):
    nj = PROMPT_LEN // MIX_ROWS
    return pl.pallas_call(
        _mixer_prompt_kernel,
        grid=(N_PROMPT_SEQ, nj),
        in_specs=[pl.BlockSpec((MIX_ROWS, D_MODEL), lambda b, j: (b * nj + j, 0))]
        + _mixer_weight_specs(2),
        out_specs=[
            pl.BlockSpec((MIX_ROWS, D_MODEL), lambda b, j: (b * nj + j, 0)),
            pl.BlockSpec((1, 2, D_B), lambda b, j: (b, 0, 0)),
        ],
        out_shape=[
            jax.ShapeDtypeStruct((T_PROMPT, D_MODEL), F32),
            jax.ShapeDtypeStruct((N_PROMPT_SEQ, 2, D_B), F32),
        ],
        scratch_shapes=[
            pltpu.VMEM((MIX_ROWS, D_A), BF16),
            pltpu.VMEM((MIX_ROWS + SUBLANES, D_B), F32),
        ],
        compiler_params=pltpu.CompilerParams(
            dimension_semantics=("arbitrary", "arbitrary"), vmem_limit_bytes=VMEM_LIMIT),
        name="mixer_prompt",
    )(x2d, *weights)


def _mixer_sample(x2d, row_block, weights, p1, p2):
    full = lambda shape: pl.BlockSpec(shape, lambda i: (0,) * len(shape))
    return pl.pallas_call(
        _mixer_sample_kernel,
        grid=(1,),
        in_specs=[pl.BlockSpec((T_SAMPLE, D_MODEL), lambda i: (row_block, 0))]
        + _mixer_weight_specs(1)
        + [full((T_SAMPLE, D_B)), full((T_SAMPLE, D_B))],
        out_specs=[full((T_SAMPLE, D_MODEL)), full((T_SAMPLE, D_A)), full((T_SAMPLE, D_B))],
        out_shape=[
            jax.ShapeDtypeStruct((T_SAMPLE, D_MODEL), F32),
            jax.ShapeDtypeStruct((T_SAMPLE, D_A), F32),
            jax.ShapeDtypeStruct((T_SAMPLE, D_B), F32),
        ],
        scratch_shapes=[
            pltpu.VMEM((T_SAMPLE, D_A), BF16),
            pltpu.VMEM((T_SAMPLE + SUBLANES, D_B), F32),
        ],
        compiler_params=pltpu.CompilerParams(
            dimension_semantics=("arbitrary",), vmem_limit_bytes=VMEM_LIMIT),
        name="mixer_sample",
    )(x2d, *weights, p1, p2)


def _cmpx(v, i, j):
    hi = jnp.maximum(v[i], v[j])
    lo = jnp.minimum(v[i], v[j])
    v[i] = hi
    v[j] = lo


def _bitonic_merge16(v):
    j = TOPK // 2
    while j >= 1:
        for i in range(TOPK):
            l = i ^ j
            if l > i:
                _cmpx(v, i, l)
        j //= 2


def _sort16(v):
    k = 2
    while k <= TOPK:
        j = k // 2
        while j >= 1:
            for i in range(TOPK):
                l = i ^ j
                if l > i:
                    if (i & k) == 0:
                        _cmpx(v, i, l)
                    else:
                        _cmpx(v, l, i)
            j //= 2
        k *= 2


def _top16(s):
    v = [s[k] for k in range(TOPK)]
    _sort16(v)
    for shift in (4, 2, 1):
        w = [pltpu.roll(v[k], shift, 0) for k in range(TOPK)]
        v = [jnp.maximum(v[k], w[TOPK - 1 - k]) for k in range(TOPK)]
        _bitonic_merge16(v)
    return v


def _sublane_allreduce(x, op):
    for shift in (4, 2, 1):
        x = op(x, pltpu.roll(x, shift, 0))
    return x


def _on_sublanes(vals, sub):
    out = vals[SUBLANES - 1]
    for r in range(SUBLANES - 2, -1, -1):
        out = jnp.where(sub == r, vals[r], out)
    return out


def _count_above(t, x, above):
    bits = []
    for step in range(4):
        width = TOPK >> (step + 1)
        pivots = [t[base + width - 1] for base in range(0, TOPK, 2 * width)]
        for bit in reversed(bits):
            pivots = [jnp.where(bit, hi, lo) for lo, hi in zip(pivots[0::2], pivots[1::2])]
        pivot = pivots[0]
        bits.append(above(pivot[None] if pivot.ndim == 2 else pivot, x))
    count = jnp.where(above(t[TOPK - 1][None], x), 1.0, 0.0)
    for step, bit in enumerate(bits):
        count = count + jnp.where(bit, float(TOPK >> (step + 1)), 0.0)
    return count


def _select_kernel(x1p_ref, x1s_ref, ln2_ref, wqt_ref, k1_ref, k2_ref,
                   h2t_ref, r2_ref, e2_ref, c1_ref, p1_ref, s1_scr, s2_scr):
    tokens = SEL_TOKENS
    is_sample = pl.program_id(0) == T_PROMPT // SEL_TOKENS
    h2 = _rms(jnp.where(is_sample, x1s_ref[...], x1p_ref[...]), ln2_ref[...])
    h2t = h2.T.astype(BF16)
    h2t_ref[...] = pltpu.bitcast(h2t, jnp.uint32)
    qt = _dot(wqt_ref[...], h2t)
    for h in range(HEADS):
        q1 = qt[h * D_QUERY:h * D_QUERY + D_HALF].astype(BF16)
        q2 = qt[h * D_QUERY + D_HALF:(h + 1) * D_QUERY].astype(BF16)
        s1_scr[h] = _dot(k1_ref[...], q1).reshape(TOPK, SUBLANES, tokens)
        s2_scr[h] = _dot(k2_ref[...], q2).reshape(TOPK, SUBLANES, tokens)

    sub = lax.broadcasted_iota(jnp.int32, (SUBLANES, LANES), 0)

    def lane_block(lb, carry):
        lanes = pl.ds(pl.multiple_of(lb * LANES, LANES), LANES)
        for h in range(HEADS):
            s1 = s1_scr[h, :, :, lanes]
            s2 = s2_scr[h, :, :, lanes]
            t1 = _top16(s1)
            t2 = _top16(s2)
            t2lo = _on_sublanes(t2[0:8], sub)
            t2hi = _on_sublanes(t2[8:16], sub)
            t1hi = _on_sublanes(t1[8:16], sub)
            cands = [t1[0] + t2lo, t1[0] + t2hi, t1[1] + t2lo]
            for a, nb in ((2, 5), (3, 4), (4, 3), (5, 2), (6, 2), (7, 2)):
                cands.append(jnp.where(sub < nb, t1[a] + t2lo, NEG))
            cands.append(t1hi + t2[0])
            cur = list(cands)
            tops = []
            for it in range(TOPK + 1):
                m = functools.reduce(jnp.maximum, cur)
                m = _sublane_allreduce(m, jnp.maximum)
                tops.append(m)
                if it < TOPK:
                    cur = [jnp.where(c == m, NEG, c) for c in cur]
            tau = 0.5 * (tops[TOPK - 1] + tops[TOPK])
            z = functools.reduce(
                jnp.add, [jnp.where(c >= tau, jnp.exp(c - tops[0]), 0.0) for c in cands])
            z = _sublane_allreduce(z, jnp.add)
            inv_z = 1.0 / z
            p1 = jnp.where(s1 >= t1[TOPK - 1][None], jnp.exp(s1 - t1[0][None]) * inv_z[None], 0.0)
            e2 = jnp.where(s2 >= t2[TOPK - 1][None], jnp.exp(s2 - t2[0][None]), 0.0)
            th = tau[None] - s1
            r2_ref[h, lb] = _count_above(t2, s2, jnp.greater)
            e2_ref[h, lb] = e2
            c1_ref[h, lb] = _count_above(t2, th, jnp.greater_equal)
            p1_ref[h, lb] = p1
        return carry

    lax.fori_loop(0, tokens // LANES, lane_block, 0)


def _select(x1_p, x1_s, ln2, wqt, k1, k2):
    nb = T_ALL // SEL_TOKENS
    n_prompt = T_PROMPT // SEL_TOKENS
    assert T_SAMPLE == SEL_TOKENS and nb == n_prompt + 1
    lb_per_step = SEL_TOKENS // LANES
    n_lb = T_ALL // LANES
    key_shape = jax.ShapeDtypeStruct((HEADS, n_lb, TOPK, SUBLANES, LANES), F32)
    key_spec = pl.BlockSpec((HEADS, lb_per_step, TOPK, SUBLANES, LANES), lambda i: (0, i, 0, 0, 0))
    scr = pltpu.VMEM((HEADS, TOPK, SUBLANES, SEL_TOKENS), F32)
    return pl.pallas_call(
        _select_kernel,
        grid=(nb,),
        in_specs=[
            pl.BlockSpec((SEL_TOKENS, D_MODEL), lambda i: (jnp.minimum(i, n_prompt - 1), 0)),
            pl.BlockSpec((SEL_TOKENS, D_MODEL), lambda i: (0, 0)),
            _const_spec((1, D_MODEL), 1),
            _const_spec((HEADS * D_QUERY, D_MODEL), 1),
            _const_spec((N_KEYS, D_HALF), 1),
            _const_spec((N_KEYS, D_HALF), 1),
        ],
        out_specs=[pl.BlockSpec((D_MODEL // 2, SEL_TOKENS), lambda i: (0, i))] + [key_spec] * 4,
        out_shape=[jax.ShapeDtypeStruct((D_MODEL // 2, T_ALL), jnp.uint32)] + [key_shape] * 4,
        scratch_shapes=[scr, scr],
        compiler_params=pltpu.CompilerParams(
            dimension_semantics=("arbitrary",), vmem_limit_bytes=VMEM_LIMIT),
        name="peer_select",
    )(x1_p, x1_s, ln2, wqt, k1, k2)


def _peer_kernel(final, h2t_ref, r2_ref, e2_ref, c1a_ref, p1a_ref, c1b_ref, p1b_ref, u_ref, vt_ref,
                 x1p_ref, x1s_ref, pep_ref, pes_ref, ln3_ref, wpg_ref, wple_ref, lnf_ref, *out_and_scratch):
    n_out = 2 if final else 1
    o_refs = out_and_scratch[:n_out]
    acc_ref, r2_scr, e2_scr, at0_ref, at1_ref, ga0_ref, ga1_ref = out_and_scratch[n_out:]
    _peer_body(final, h2t_ref, r2_ref, e2_ref, c1a_ref, p1a_ref, c1b_ref, p1b_ref, u_ref, vt_ref,
               x1p_ref, x1s_ref, pep_ref, pes_ref, ln3_ref, wpg_ref, wple_ref, lnf_ref, o_refs,
               acc_ref, r2_scr, e2_scr, at0_ref, at1_ref, ga0_ref, ga1_ref)


def _peer_body(final, h2t_ref, r2_ref, e2_ref, c1a_ref, p1a_ref, c1b_ref, p1b_ref, u_ref, vt_ref,
               x1p_ref, x1s_ref, pep_ref, pes_ref, ln3_ref, wpg_ref, wple_ref, lnf_ref, o_refs,
               acc_ref, r2_scr, e2_scr, at0_ref, at1_ref, ga0_ref, ga1_ref):
    j = pl.program_id(0)
    phase = j % PEER_STEPS_PER_BLOCK
    rows_half = HALF_EXPERTS // N_KEYS
    vregs = GATE_KEYS // BF16_ROWS

    def load_keys():
        for h in range(HEADS):
            r2_scr[h] = r2_ref[h].astype(BF16)
            e2_scr[h] = e2_ref[h].astype(BF16)

    @pl.when(j == 0)
    def _():
        at1_ref[...] = jnp.zeros_like(at1_ref)
        ga0_ref[...] = jnp.zeros_like(ga0_ref)
        ga1_ref[...] = jnp.zeros_like(ga1_ref)
        load_keys()

    @pl.when((phase == 1) | (j == 0))
    def _():
        acc_ref[...] = jnp.zeros_like(acc_ref)

    def lane_bcast(ref, h, lb, r):
        return jnp.broadcast_to(ref[h, lb, r:r + 1, :], (BF16_ROWS, LANES)).astype(BF16)

    def gate_tile(at_ref, ga_ref, c1_ref, p1_ref, row0, lb, rg, kq):
        lanes = slice(lb * LANES, (lb + 1) * LANES)
        keys = [slice(kq * GATE_KEYS + k * BF16_ROWS, kq * GATE_KEYS + (k + 1) * BF16_ROWS)
                for k in range(vregs)]
        key_rows = range(rg * GATE_ROWS, (rg + 1) * GATE_ROWS)
        g = {r: [None] * vregs for r in key_rows}
        for h in range(HEADS):
            r2 = [r2_scr[h, lb, ks, :] for ks in keys]
            e2 = [e2_scr[h, lb, ks, :] for ks in keys]
            for r in key_rows:
                c = lane_bcast(c1_ref, h, lb, row0 + r)
                p = lane_bcast(p1_ref, h, lb, row0 + r)
                for k in range(vregs):
                    term = jnp.where(r2[k] < c, e2[k], jnp.zeros_like(e2[k])) * p
                    g[r][k] = term if h == 0 else g[r][k] + term
        for r in key_rows:
            for k in range(vregs):
                rows = slice(r * N_KEYS + keys[k].start, r * N_KEYS + keys[k].stop)
                a = at_ref[rows, lanes].astype(BF16)
                ga_ref[rows, lanes] = g[r][k] * jax.nn.gelu(a)

    def half_step(half, at_new, at_cur, ga_new, ga_old, c1_ref, p1_ref, row0):
        u_rows = slice(half * HALF_EXPERTS // 2, (half + 1) * HALF_EXPERTS // 2)
        v_cols = slice(half * HALF_EXPERTS, (half + 1) * HALF_EXPERTS)
        def project(m, cols):
            rows = slice(m * HALF_EXPERTS // MXU_SPLIT, (m + 1) * HALF_EXPERTS // MXU_SPLIT)
            u_part = slice(u_rows.start + rows.start // 2, u_rows.start + rows.stop // 2)
            at_new[rows, cols] = _dot(pltpu.bitcast(u_ref[u_part, :], BF16),
                                      pltpu.bitcast(h2t_ref[:, cols], BF16))

        def accumulate(m, cols):
            rows = slice(m * D_MODEL // MXU_SPLIT, (m + 1) * D_MODEL // MXU_SPLIT)
            v_part = slice(rows.start // 2, rows.stop // 2)
            acc_ref[rows, cols] += _dot(pltpu.bitcast(vt_ref[v_part, v_cols], BF16), ga_old[:, cols])

        halves = [slice(n * PEER_TOKENS // 2, (n + 1) * PEER_TOKENS // 2) for n in range(2)]
        pieces = [(stage, m, cols) for stage in (project, accumulate) for cols in halves
                  for m in range(MXU_SPLIT)]
        tiles = [(lb, rg, kq) for lb in range(PEER_TOKENS // LANES)
                 for rg in range(rows_half // GATE_ROWS) for kq in range(N_KEYS // GATE_KEYS)]
        assert sum(TILES_AROUND_PIECE) == len(tiles) and len(TILES_AROUND_PIECE) == len(pieces)
        done = 0
        for (stage, m, cols), count in zip(pieces, TILES_AROUND_PIECE):
            mine = tiles[done:done + count]
            done += count
            for lb, rg, kq in mine[:count // 2]:
                gate_tile(at_cur, ga_new, c1_ref, p1_ref, row0, lb, rg, kq)
            stage(m, cols)
            for lb, rg, kq in mine[count // 2:]:
                gate_tile(at_cur, ga_new, c1_ref, p1_ref, row0, lb, rg, kq)

    half_step(0, at0_ref, at1_ref, ga1_ref, ga0_ref, c1a_ref, p1a_ref, rows_half % SUBLANES)

    @pl.when(phase == 0)
    def _():
        load_keys()

    half_step(1, at1_ref, at0_ref, ga0_ref, ga1_ref, c1b_ref, p1b_ref, 0)

    @pl.when((phase == 0) & (j > 0))
    def _():
        is_sample = (j - 1) // PEER_STEPS_PER_BLOCK == T_PROMPT // PEER_TOKENS
        pe = jnp.where(is_sample, pes_ref[...], pep_ref[...])
        x2 = jnp.where(is_sample, x1s_ref[...], x1p_ref[...]) + acc_ref[...].T
        h3 = _rms(x2, ln3_ref[...]).astype(BF16)
        gate = _sigmoid(_dot(h3, wpg_ref[...]))
        x3 = x2 + gate * _dot(pe.astype(BF16), wple_ref[...])
        if final:
            y = _rms(x3, lnf_ref[...])

            @pl.when(jnp.logical_not(is_sample))
            def _():
                o_refs[0][...] = y

            @pl.when(is_sample)
            def _():
                o_refs[1][...] = y
        else:
            o_refs[0][...] = x3


def _peer(final, layer, h2t, r2, e2, c1, p1, u_pk, vt_pk, x1_p, x1_s, pe_p, pe_s, ln3, wpg, wple, lnf):
    nt = T_ALL // PEER_TOKENS
    n_prompt = T_PROMPT // PEER_TOKENS
    assert T_SAMPLE == PEER_TOKENS and nt == n_prompt + 1
    tiles_per_block = N_EXPERTS // HALF_EXPERTS
    n_tiles = nt * tiles_per_block
    assert tiles_per_block == 2 * PEER_STEPS_PER_BLOCK
    rows_half = HALF_EXPERTS // N_KEYS
    assert SUBLANES % rows_half == 0
    n_steps = n_tiles // 2 + 1
    lb_per_step = PEER_TOKENS // LANES

    prev_step = lambda j: jnp.maximum(j - 1, 0)
    tok_spec = lambda width: pl.BlockSpec(
        (PEER_TOKENS, width), lambda j: (prev_step(j) // PEER_STEPS_PER_BLOCK, 0))
    cur_block = lambda j: jnp.minimum(j // PEER_STEPS_PER_BLOCK, nt - 1)
    key_spec = pl.BlockSpec((HEADS, lb_per_step, N_KEYS, LANES), lambda j: (0, cur_block(j), 0, 0))

    def row_spec(tile_of_step):
        def index(j):
            s = jnp.clip(tile_of_step(j), 0, n_tiles - 1)
            return (0, s // tiles_per_block, (s % tiles_per_block) * rows_half // SUBLANES, 0, 0)
        return pl.BlockSpec((HEADS, lb_per_step, None, SUBLANES, LANES), index)

    rows_a = row_spec(lambda j: 2 * j - 1)
    rows_b = row_spec(lambda j: 2 * j)
    const = lambda shape: pl.BlockSpec(shape, lambda j: (0,) * len(shape), pipeline_mode=pl.Buffered(1))
    prompt_block = lambda j: jnp.minimum(prev_step(j) // PEER_STEPS_PER_BLOCK, n_prompt - 1)
    if final:
        out_specs = [pl.BlockSpec((PEER_TOKENS, D_MODEL), lambda j: (prompt_block(j), 0)),
                     pl.BlockSpec((PEER_TOKENS, D_MODEL), lambda j: (0, 0))]
        out_shape = [jax.ShapeDtypeStruct((T_PROMPT, D_MODEL), F32),
                     jax.ShapeDtypeStruct((T_SAMPLE, D_MODEL), F32)]
    else:
        out_specs = [tok_spec(D_MODEL)]
        out_shape = [jax.ShapeDtypeStruct((T_ALL, D_MODEL), F32)]
    return pl.pallas_call(
        functools.partial(_peer_kernel, final),
        grid=(n_steps,),
        in_specs=[
            pl.BlockSpec((D_MODEL // 2, PEER_TOKENS), lambda j: (0, cur_block(j))),
            key_spec, key_spec, rows_a, rows_a, rows_b, rows_b,
            pl.BlockSpec((None, HALF_EXPERTS, D_MODEL), lambda j: (layer, j % PEER_STEPS_PER_BLOCK, 0)),
            pl.BlockSpec((None, D_MODEL // 2, 2 * HALF_EXPERTS),
                         lambda j: (layer, 0, prev_step(j) % PEER_STEPS_PER_BLOCK)),
            pl.BlockSpec((PEER_TOKENS, D_MODEL), lambda j: (prompt_block(j), 0)),
            pl.BlockSpec((PEER_TOKENS, D_MODEL), lambda j: (0, 0), pipeline_mode=pl.Buffered(1)),
            pl.BlockSpec((None, PEER_TOKENS, D_PLE), lambda j: (layer, prompt_block(j), 0)),
            pl.BlockSpec((None, PEER_TOKENS, D_PLE), lambda j: (layer, 0, 0),
                         pipeline_mode=pl.Buffered(1)),
            const((1, D_MODEL)), const((D_MODEL, D_MODEL)), const((D_PLE, D_MODEL)), const((1, D_MODEL)),
        ],
        out_specs=out_specs,
        out_shape=out_shape,
        scratch_shapes=[pltpu.VMEM((D_MODEL, PEER_TOKENS), F32)]
        + [pltpu.VMEM((HEADS, lb_per_step, N_KEYS, LANES), BF16)] * 2
        + [pltpu.VMEM((HALF_EXPERTS, PEER_TOKENS), F32)] * 2
        + [pltpu.VMEM((HALF_EXPERTS, PEER_TOKENS), BF16)] * 2,
        compiler_params=pltpu.CompilerParams(
            dimension_semantics=("arbitrary",), vmem_limit_bytes=VMEM_LIMIT),
        name="peer_dense",
    )(h2t, r2, e2, c1, p1, c1, p1, u_pk, vt_pk, x1_p, x1_s, pe_p, pe_s, ln3, wpg, wple, lnf)


def _pack_tables_kernel(u_ref, v_ref, upk_ref, vtpk_ref):
    upk_ref[...] = pltpu.bitcast(u_ref[...].astype(BF16), jnp.uint32)
    vtpk_ref[...] = pltpu.bitcast(v_ref[...].T.astype(BF16), jnp.uint32)


def _pack_tables(u_tab, v_tab):
    nb = N_EXPERTS // PACK_EXPERTS
    in_spec = pl.BlockSpec((None, PACK_EXPERTS, D_MODEL), lambda l, e: (l, e, 0))
    return pl.pallas_call(
        _pack_tables_kernel,
        grid=(DEPTH, nb),
        in_specs=[in_spec, in_spec],
        out_specs=[
            pl.BlockSpec((None, PACK_EXPERTS // 2, D_MODEL), lambda l, e: (l, e, 0)),
            pl.BlockSpec((None, D_MODEL // 2, PACK_EXPERTS), lambda l, e: (l, 0, e)),
        ],
        out_shape=[
            jax.ShapeDtypeStruct((DEPTH, N_EXPERTS // 2, D_MODEL), jnp.uint32),
            jax.ShapeDtypeStruct((DEPTH, D_MODEL // 2, N_EXPERTS), jnp.uint32),
        ],
        compiler_params=pltpu.CompilerParams(
            dimension_semantics=("arbitrary", "arbitrary"), vmem_limit_bytes=VMEM_LIMIT),
        name="pack_tables",
    )(u_tab, v_tab)


def kernel(x_prompt, x_sample, state_conv, p_prompt, p_sample, ln1, w_in, ln_v, w_s, b_s, conv_w,
           w_a, w_b, w_o, ln2, w_q, k1, k2, u_tab, v_tab, ln3, w_pg, w_ple, ln_f):
    assert x_prompt.shape == (N_PROMPT_SEQ, PROMPT_LEN, D_MODEL)
    assert x_sample.shape == (N_SAMPLE_SEQ, SAMPLE_LEN, D_MODEL)
    assert u_tab.shape == (DEPTH, N_EXPERTS, D_MODEL)

    x_p = x_prompt.reshape(T_PROMPT, D_MODEL)
    x_s = x_sample.reshape(T_SAMPLE, D_MODEL)
    sample_block = 0
    seqs_per_chunk = CHUNK // SAMPLE_LEN
    eye = jnp.eye(seqs_per_chunk, dtype=F32)
    lnf = ln_f.reshape(1, D_MODEL)
    u_pk, vt_pk = _pack_tables(u_tab, v_tab)
    pe_p = p_prompt.reshape(DEPTH, T_PROMPT, D_PLE)
    pe_s = p_sample.reshape(DEPTH, T_SAMPLE, D_PLE)
    conv_prompt, conv_sample, v_sample = [], [], []
    y = None

    for i in range(DEPTH):
        row = lambda a: a[i].reshape(1, -1)
        ws_p = w_s[i].astype(BF16)
        corner = w_s[i][:, :SAMPLE_LEN, :SAMPLE_LEN]
        ws_s = (eye[None, :, None, :, None] * corner[:, None, :, None, :]).reshape(
            N_GROUPS, CHUNK, CHUNK).astype(BF16)
        bias_p = jnp.repeat(b_s[i].T, GROUP, axis=1)
        bias_s = jnp.tile(jnp.repeat(b_s[i][:, :SAMPLE_LEN].T, GROUP, axis=1), (seqs_per_chunk, 1))
        shared = (conv_w[i], w_a[i].astype(BF16), w_b[i].astype(BF16), w_o[i].astype(BF16))
        front = (row(ln1), w_in[i].astype(BF16), row(ln_v))
        w_prompt = front + (ws_p, bias_p) + shared
        w_sample = front + (ws_s, bias_s) + shared

        prev = state_conv[i]
        zero = jnp.zeros((N_SAMPLE_SEQ, 1, D_B), F32)
        tap1 = jnp.concatenate([prev[:, 1:2], zero, zero, zero], axis=1).reshape(T_SAMPLE, D_B)
        tap2 = jnp.concatenate([prev[:, 0:1], prev[:, 1:2], zero, zero], axis=1).reshape(T_SAMPLE, D_B)

        x1_p, tail_p = _mixer_prompt(x_p, w_prompt)
        x1_s, v_s, cx_s = _mixer_sample(x_s, sample_block, w_sample, tap1, tap2)
        conv_prompt.append(tail_p)
        conv_sample.append(cx_s.reshape(N_SAMPLE_SEQ, SAMPLE_LEN, D_B)[:, SAMPLE_LEN - 2:])
        v_sample.append(v_s.reshape(N_SAMPLE_SEQ, SAMPLE_LEN, D_A))

        h2t, r2, e2, c1, p1 = _select(x1_p, x1_s, row(ln2), w_q[i].T.astype(BF16),
                                      k1[i].astype(BF16), k2[i].astype(BF16))
        final = i == DEPTH - 1
        r2 = r2.reshape(HEADS, T_ALL // LANES, N_KEYS, LANES)
        e2 = e2.reshape(HEADS, T_ALL // LANES, N_KEYS, LANES)
        y = _peer(final, i, h2t, r2, e2, c1, p1, u_pk, vt_pk,
                  x1_p, x1_s, pe_p, pe_s, row(ln3), w_pg[i].astype(BF16), w_ple[i].astype(BF16), lnf)
        x_p = y[0]
        x_s = y[0]
        sample_block = T_PROMPT // T_SAMPLE

    y_prompt = y[0].reshape(N_PROMPT_SEQ, PROMPT_LEN, D_MODEL)
    y_sample = y[1].reshape(N_SAMPLE_SEQ, SAMPLE_LEN, D_MODEL)
    return (y_prompt, y_sample, jnp.stack(conv_prompt), jnp.stack(conv_sample), jnp.stack(v_sample))
```

```python
import functools

import jax
import jax.numpy as jnp
from jax import lax
from jax.experimental import pallas as pl
from jax.experimental.pallas import tpu as pltpu

F32 = jnp.float32
BF16 = jnp.bfloat16

D_MODEL = 1024
D_A = 1024
D_B = 1024
D_PLE = 256
CHUNK = 128
N_GROUPS = 8
GROUP = D_A // N_GROUPS
N_KEYS = 128
N_EXPERTS = N_KEYS * N_KEYS
HEADS = 8
D_QUERY = 256
D_HALF = 128
TOPK = 16
EPS = 1e-6
DEPTH = 2

N_PROMPT_SEQ = 8
PROMPT_LEN = 2048
N_SAMPLE_SEQ = 128
SAMPLE_LEN = 4
T_PROMPT = N_PROMPT_SEQ * PROMPT_LEN
T_SAMPLE = N_SAMPLE_SEQ * SAMPLE_LEN
T_ALL = T_PROMPT + T_SAMPLE

LANES = 128
SUBLANES = 8
BF16_ROWS = 16
MIX_ROWS = 512
SEL_TOKENS = 512
PEER_TOKENS = 512
HALF_EXPERTS = 1024
PEER_STEPS_PER_BLOCK = N_EXPERTS // (2 * HALF_EXPERTS)
GATE_ROWS = 4
GATE_KEYS = 64
PACK_EXPERTS = 1024
VMEM_LIMIT = 60 * 1024 * 1024
NEG = -1e30


def _rms(x, g):
    ms = jnp.mean(x * x, axis=-1, keepdims=True)
    return x * lax.rsqrt(ms + EPS) * g


def _sigmoid(x):
    return 1.0 / (1.0 + jnp.exp(-x))


def _dot(a, b):
    return jnp.dot(a, b, preferred_element_type=F32)


def _const_spec(shape, grid_rank):
    zeros = (0,) * len(shape)
    if grid_rank == 1:
        return pl.BlockSpec(shape, lambda i: zeros, pipeline_mode=pl.Buffered(1))
    return pl.BlockSpec(shape, lambda i, j: zeros, pipeline_mode=pl.Buffered(1))


def _mixer_front(x, ln1_ref, win_ref, lnv_ref, ws_ref, bias_ref, yin_ref, rows):
    h = _rms(x, ln1_ref[...]).astype(BF16)

    def proj(k):
        return _dot(h, win_ref[:, k * D_MODEL:(k + 1) * D_MODEL])

    u = jax.nn.gelu(proj(0))
    v = _rms(jax.nn.gelu(proj(1)), lnv_ref[...])
    vb = v.astype(BF16)
    r_i = lax.broadcasted_iota(jnp.int32, (CHUNK, CHUNK), 0)
    c_i = lax.broadcasted_iota(jnp.int32, (CHUNK, CHUNK), 1)
    causal = r_i >= c_i
    for g in range(N_GROUPS):
        w = jnp.where(causal, ws_ref[g], jnp.zeros((), BF16))
        cols = slice(g * GROUP, (g + 1) * GROUP)
        for c in range(rows // CHUNK):
            rws = slice(c * CHUNK, (c + 1) * CHUNK)
            s = _dot(w, vb[rws, cols]) + bias_ref[:, cols]
            yin_ref[rws, cols] = (u[rws, cols] * s).astype(BF16)
    return h, v, proj


def _mixer_back(x, proj, yc, wa_ref, wb_ref, wo_ref, yin_ref):
    ya = _dot(yin_ref[...], wa_ref[...])
    yb = _dot((proj(2) * yc).astype(BF16), wb_ref[...])
    mix = _sigmoid(proj(5)) * ya + _sigmoid(proj(6)) * yb
    return x + _dot(mix.astype(BF16), wo_ref[...])


def _mixer_prompt_kernel(x_ref, ln1_ref, win_ref, lnv_ref, ws_ref, bias_ref, cw_ref,
                         wa_ref, wb_ref, wo_ref, x1_ref, tail_ref, yin_ref, cbuf_ref):
    rows = MIX_ROWS
    x = x_ref[...]
    h, v, proj = _mixer_front(x, ln1_ref, win_ref, lnv_ref, ws_ref, bias_ref, yin_ref, rows)
    cx = proj(3) * proj(4)

    @pl.when(pl.program_id(1) == 0)
    def _():
        cbuf_ref[0:SUBLANES, :] = jnp.zeros((SUBLANES, D_B), F32)

    cbuf_ref[SUBLANES:SUBLANES + rows, :] = cx
    c1 = cbuf_ref[SUBLANES - 1:SUBLANES - 1 + rows, :]
    c2 = cbuf_ref[SUBLANES - 2:SUBLANES - 2 + rows, :]
    yc = cw_ref[0:1, :] * c2 + cw_ref[1:2, :] * c1 + cw_ref[2:3, :] * cx
    cbuf_ref[0:SUBLANES, :] = cx[rows - SUBLANES:rows, :]
    tail_ref[0] = cx[rows - 2:rows, :]
    x1_ref[...] = _mixer_back(x, proj, yc, wa_ref, wb_ref, wo_ref, yin_ref)


def _mixer_sample_kernel(x_ref, ln1_ref, win_ref, lnv_ref, ws_ref, bias_ref, cw_ref,
                         wa_ref, wb_ref, wo_ref, p1_ref, p2_ref,
                         x1_ref, v_ref, cx_ref, yin_ref, cbuf_ref):
    rows = T_SAMPLE
    x = x_ref[...]
    h, v, proj = _mixer_front(x, ln1_ref, win_ref, lnv_ref, ws_ref, bias_ref, yin_ref, rows)
    v_ref[...] = v
    cx = proj(3) * proj(4)
    cx_ref[...] = cx
    cbuf_ref[0:SUBLANES, :] = jnp.zeros((SUBLANES, D_B), F32)
    cbuf_ref[SUBLANES:SUBLANES + rows, :] = cx
    pos = lax.broadcasted_iota(jnp.int32, (rows, D_B), 0) & (SAMPLE_LEN - 1)
    c1 = jnp.where(pos >= 1, cbuf_ref[SUBLANES - 1:SUBLANES - 1 + rows, :], p1_ref[...])
    c2 = jnp.where(pos >= 2, cbuf_ref[SUBLANES - 2:SUBLANES - 2 + rows, :], p2_ref[...])
    yc = cw_ref[0:1, :] * c2 + cw_ref[1:2, :] * c1 + cw_ref[2:3, :] * cx
    x1_ref[...] = _mixer_back(x, proj, yc, wa_ref, wb_ref, wo_ref, yin_ref)


def _mixer_weight_specs(grid_rank):
    cs = functools.partial(_const_spec, grid_rank=grid_rank)
    return [
        cs((1, D_MODEL)),
        cs((D_MODEL, 7 * D_MODEL)),
        cs((1, D_A)),
        cs((N_GROUPS, CHUNK, CHUNK)),
        cs((CHUNK, D_A)),
        cs((3, D_B)),
        cs((D_A, D_MODEL)),
        cs((D_B, D_MODEL)),
        cs((D_MODEL, D_MODEL)),
    ]


def _mixer_prompt(x2d, weights):
    nj = PROMPT_LEN // MIX_ROWS
    return pl.pallas_call(
        _mixer_prompt_kernel,
        grid=(N_PROMPT_SEQ, nj),
        in_specs=[pl.BlockSpec((MIX_ROWS, D_MODEL), lambda b, j: (b * nj + j, 0))]
        + _mixer_weight_specs(2),
        out_specs=[
            pl.BlockSpec((MIX_ROWS, D_MODEL), lambda b, j: (b * nj + j, 0)),
            pl.BlockSpec((1, 2, D_B), lambda b, j: (b, 0, 0)),
        ],
        out_shape=[
            jax.ShapeDtypeStruct((T_PROMPT, D_MODEL), F32),
            jax.ShapeDtypeStruct((N_PROMPT_SEQ, 2, D_B), F32),
        ],
        scratch_shapes=[
            pltpu.VMEM((MIX_ROWS, D_A), BF16),
            pltpu.VMEM((MIX_ROWS + SUBLANES, D_B), F32),
        ],
        compiler_params=pltpu.CompilerParams(
            dimension_semantics=("arbitrary", "arbitrary"), vmem_limit_bytes=VMEM_LIMIT),
        name="mixer_prompt",
    )(x2d, *weights)


def _mixer_sample(x2d, row_block, weights, p1, p2):
    full = lambda shape: pl.BlockSpec(shape, lambda i: (0,) * len(shape))
    return pl.pallas_call(
        _mixer_sample_kernel,
        grid=(1,),
        in_specs=[pl.BlockSpec((T_SAMPLE, D_MODEL), lambda i: (row_block, 0))]
        + _mixer_weight_specs(1)
        + [full((T_SAMPLE, D_B)), full((T_SAMPLE, D_B))],
        out_specs=[full((T_SAMPLE, D_MODEL)), full((T_SAMPLE, D_A)), full((T_SAMPLE, D_B))],
        out_shape=[
            jax.ShapeDtypeStruct((T_SAMPLE, D_MODEL), F32),
            jax.ShapeDtypeStruct((T_SAMPLE, D_A), F32),
            jax.ShapeDtypeStruct((T_SAMPLE, D_B), F32),
        ],
        scratch_shapes=[
            pltpu.VMEM((T_SAMPLE, D_A), BF16),
            pltpu.VMEM((T_SAMPLE + SUBLANES, D_B), F32),
        ],
        compiler_params=pltpu.CompilerParams(
            dimension_semantics=("arbitrary",), vmem_limit_bytes=VMEM_LIMIT),
        name="mixer_sample",
    )(x2d, *weights, p1, p2)


def _cmpx(v, i, j):
    hi = jnp.maximum(v[i], v[j])
    lo = jnp.minimum(v[i], v[j])
    v[i] = hi
    v[j] = lo


def _bitonic_merge16(v):
    j = TOPK // 2
    while j >= 1:
        for i in range(TOPK):
            l = i ^ j
            if l > i:
                _cmpx(v, i, l)
        j //= 2


def _sort16(v):
    k = 2
    while k <= TOPK:
        j = k // 2
        while j >= 1:
            for i in range(TOPK):
                l = i ^ j
                if l > i:
                    if (i & k) == 0:
                        _cmpx(v, i, l)
                    else:
                        _cmpx(v, l, i)
            j //= 2
        k *= 2


def _top16(s):
    v = [s[k] for k in range(TOPK)]
    _sort16(v)
    for shift in (4, 2, 1):
        w = [pltpu.roll(v[k], shift, 0) for k in range(TOPK)]
        v = [jnp.maximum(v[k], w[TOPK - 1 - k]) for k in range(TOPK)]
        _bitonic_merge16(v)
    return v


def _sublane_allreduce(x, op):
    for shift in (4, 2, 1):
        x = op(x, pltpu.roll(x, shift, 0))
    return x


def _on_sublanes(vals, sub):
    out = vals[SUBLANES - 1]
    for r in range(SUBLANES - 2, -1, -1):
        out = jnp.where(sub == r, vals[r], out)
    return out


def _count_above(t, x, above):
    bits = []
    for step in range(4):
        width = TOPK >> (step + 1)
        pivots = [t[base + width - 1] for base in range(0, TOPK, 2 * width)]
        for bit in reversed(bits):
            pivots = [jnp.where(bit, hi, lo) for lo, hi in zip(pivots[0::2], pivots[1::2])]
        pivot = pivots[0]
        bits.append(above(pivot[None] if pivot.ndim == 2 else pivot, x))
    count = jnp.where(above(t[TOPK - 1][None], x), 1.0, 0.0)
    for step, bit in enumerate(bits):
        count = count + jnp.where(bit, float(TOPK >> (step + 1)), 0.0)
    return count


def _select_kernel(x1p_ref, x1s_ref, ln2_ref, wqt_ref, k1_ref, k2_ref,
                   h2t_ref, r2_ref, e2_ref, c1_ref, p1_ref, s1_scr, s2_scr):
    tokens = SEL_TOKENS
    is_sample = pl.program_id(0) == T_PROMPT // SEL_TOKENS
    h2 = _rms(jnp.where(is_sample, x1s_ref[...], x1p_ref[...]), ln2_ref[...])
    h2t = h2.T.astype(BF16)
    h2t_ref[...] = pltpu.bitcast(h2t, jnp.uint32)
    qt = _dot(wqt_ref[...], h2t)
    for h in range(HEADS):
        q1 = qt[h * D_QUERY:h * D_QUERY + D_HALF].astype(BF16)
        q2 = qt[h * D_QUERY + D_HALF:(h + 1) * D_QUERY].astype(BF16)
        s1_scr[h] = _dot(k1_ref[...], q1).reshape(TOPK, SUBLANES, tokens)
        s2_scr[h] = _dot(k2_ref[...], q2).reshape(TOPK, SUBLANES, tokens)

    sub = lax.broadcasted_iota(jnp.int32, (SUBLANES, LANES), 0)

    def lane_block(lb, carry):
        lanes = pl.ds(pl.multiple_of(lb * LANES, LANES), LANES)
        for h in range(HEADS):
            s1 = s1_scr[h, :, :, lanes]
            s2 = s2_scr[h, :, :, lanes]
            t1 = _top16(s1)
            t2 = _top16(s2)
            t2lo = _on_sublanes(t2[0:8], sub)
            t2hi = _on_sublanes(t2[8:16], sub)
            t1hi = _on_sublanes(t1[8:16], sub)
            cands = [t1[0] + t2lo, t1[0] + t2hi, t1[1] + t2lo]
            for a, nb in ((2, 5), (3, 4), (4, 3), (5, 2), (6, 2), (7, 2)):
                cands.append(jnp.where(sub < nb, t1[a] + t2lo, NEG))
            cands.append(t1hi + t2[0])
            cur = list(cands)
            tops = []
            for it in range(TOPK + 1):
                m = functools.reduce(jnp.maximum, cur)
                m = _sublane_allreduce(m, jnp.maximum)
                tops.append(m)
                if it < TOPK:
                    cur = [jnp.where(c == m, NEG, c) for c in cur]
            tau = 0.5 * (tops[TOPK - 1] + tops[TOPK])
            z = functools.reduce(
                jnp.add, [jnp.where(c >= tau, jnp.exp(c - tops[0]), 0.0) for c in cands])
            z = _sublane_allreduce(z, jnp.add)
            inv_z = 1.0 / z
            p1 = jnp.where(s1 >= t1[TOPK - 1][None], jnp.exp(s1 - t1[0][None]) * inv_z[None], 0.0)
            e2 = jnp.where(s2 >= t2[TOPK - 1][None], jnp.exp(s2 - t2[0][None]), 0.0)
            th = tau[None] - s1
            r2_ref[h, lb] = _count_above(t2, s2, jnp.greater)
            e2_ref[h, lb] = e2
            c1_ref[h, lb] = _count_above(t2, th, jnp.greater_equal)
            p1_ref[h, lb] = p1
        return carry

    lax.fori_loop(0, tokens // LANES, lane_block, 0)


def _select(x1_p, x1_s, ln2, wqt, k1, k2):
    nb = T_ALL // SEL_TOKENS
    n_prompt = T_PROMPT // SEL_TOKENS
    assert T_SAMPLE == SEL_TOKENS and nb == n_prompt + 1
    lb_per_step = SEL_TOKENS // LANES
    n_lb = T_ALL // LANES
    key_shape = jax.ShapeDtypeStruct((HEADS, n_lb, TOPK, SUBLANES, LANES), F32)
    key_spec = pl.BlockSpec((HEADS, lb_per_step, TOPK, SUBLANES, LANES), lambda i: (0, i, 0, 0, 0))
    scr = pltpu.VMEM((HEADS, TOPK, SUBLANES, SEL_TOKENS), F32)
    return pl.pallas_call(
        _select_kernel,
        grid=(nb,),
        in_specs=[
            pl.BlockSpec((SEL_TOKENS, D_MODEL), lambda i: (jnp.minimum(i, n_prompt - 1), 0)),
            pl.BlockSpec((SEL_TOKENS, D_MODEL), lambda i: (0, 0)),
            _const_spec((1, D_MODEL), 1),
            _const_spec((HEADS * D_QUERY, D_MODEL), 1),
            _const_spec((N_KEYS, D_HALF), 1),
            _const_spec((N_KEYS, D_HALF), 1),
        ],
        out_specs=[pl.BlockSpec((D_MODEL // 2, SEL_TOKENS), lambda i: (0, i))] + [key_spec] * 4,
        out_shape=[jax.ShapeDtypeStruct((D_MODEL // 2, T_ALL), jnp.uint32)] + [key_shape] * 4,
        scratch_shapes=[scr, scr],
        compiler_params=pltpu.CompilerParams(
            dimension_semantics=("arbitrary",), vmem_limit_bytes=VMEM_LIMIT),
        name="peer_select",
    )(x1_p, x1_s, ln2, wqt, k1, k2)


def _peer_kernel(final, h2t_ref, r2_ref, e2_ref, c1a_ref, p1a_ref, c1b_ref, p1b_ref, u_ref, vt_ref,
                 x1p_ref, x1s_ref, pep_ref, pes_ref, ln3_ref, wpg_ref, wple_ref, lnf_ref, *out_and_scratch):
    n_out = 2 if final else 1
    o_refs = out_and_scratch[:n_out]
    acc_ref, r2_scr, e2_scr, at0_ref, at1_ref, ga0_ref, ga1_ref = out_and_scratch[n_out:]
    _peer_body(final, h2t_ref, r2_ref, e2_ref, c1a_ref, p1a_ref, c1b_ref, p1b_ref, u_ref, vt_ref,
               x1p_ref, x1s_ref, pep_ref, pes_ref, ln3_ref, wpg_ref, wple_ref, lnf_ref, o_refs,
               acc_ref, r2_scr, e2_scr, at0_ref, at1_ref, ga0_ref, ga1_ref)


def _peer_body(final, h2t_ref, r2_ref, e2_ref, c1a_ref, p1a_ref, c1b_ref, p1b_ref, u_ref, vt_ref,
               x1p_ref, x1s_ref, pep_ref, pes_ref, ln3_ref, wpg_ref, wple_ref, lnf_ref, o_refs,
               acc_ref, r2_scr, e2_scr, at0_ref, at1_ref, ga0_ref, ga1_ref):
    j = pl.program_id(0)
    phase = j % PEER_STEPS_PER_BLOCK
    rows_half = HALF_EXPERTS // N_KEYS
    vregs = GATE_KEYS // BF16_ROWS

    def load_keys():
        for h in range(HEADS):
            r2_scr[h] = r2_ref[h].astype(BF16)
            e2_scr[h] = e2_ref[h].astype(BF16)

    @pl.when(j == 0)
    def _():
        at1_ref[...] = jnp.zeros_like(at1_ref)
        ga0_ref[...] = jnp.zeros_like(ga0_ref)
        ga1_ref[...] = jnp.zeros_like(ga1_ref)
        load_keys()

    @pl.when((phase == 1) | (j == 0))
    def _():
        acc_ref[...] = jnp.zeros_like(acc_ref)

    def lane_bcast(ref, h, lb, r):
        return jnp.broadcast_to(ref[h, lb, r:r + 1, :], (BF16_ROWS, LANES)).astype(BF16)

    def gate_tile(at_ref, ga_ref, c1_ref, p1_ref, row0, lb, rg, kq):
        lanes = slice(lb * LANES, (lb + 1) * LANES)
        keys = [slice(kq * GATE_KEYS + k * BF16_ROWS, kq * GATE_KEYS + (k + 1) * BF16_ROWS)
                for k in range(vregs)]
        key_rows = range(rg * GATE_ROWS, (rg + 1) * GATE_ROWS)
        g = {r: [None] * vregs for r in key_rows}
        for h in range(HEADS):
            r2 = [r2_scr[h, lb, ks, :] for ks in keys]
            e2 = [e2_scr[h, lb, ks, :] for ks in keys]
            for r in key_rows:
                c = lane_bcast(c1_ref, h, lb, row0 + r)
                p = lane_bcast(p1_ref, h, lb, row0 + r)
                for k in range(vregs):
                    term = jnp.where(r2[k] < c, e2[k], jnp.zeros_like(e2[k])) * p
                    g[r][k] = term if h == 0 else g[r][k] + term
        for r in key_rows:
            for k in range(vregs):
                rows = slice(r * N_KEYS + keys[k].start, r * N_KEYS + keys[k].stop)
                a = at_ref[rows, lanes].astype(BF16)
                ga_ref[rows, lanes] = g[r][k] * jax.nn.gelu(a)

    def half_step(half, at_new, at_cur, ga_new, ga_old, c1_ref, p1_ref, row0):
        u_rows = slice(half * HALF_EXPERTS // 2, (half + 1) * HALF_EXPERTS // 2)
        v_cols = slice(half * HALF_EXPERTS, (half + 1) * HALF_EXPERTS)
        def project(cols):
            at_new[:, cols] = _dot(pltpu.bitcast(u_ref[u_rows, :], BF16),
                                   pltpu.bitcast(h2t_ref[:, cols], BF16))

        def accumulate(cols):
            acc_ref[:, cols] += _dot(pltpu.bitcast(vt_ref[:, v_cols], BF16), ga_old[:, cols])

        halves = [slice(n * PEER_TOKENS // 2, (n + 1) * PEER_TOKENS // 2) for n in range(2)]
        pieces = [(stage, cols) for stage in (project, accumulate) for cols in halves]
        tiles = [(lb, rg, kq) for lb in range(PEER_TOKENS // LANES)
                 for rg in range(rows_half // GATE_ROWS) for kq in range(N_KEYS // GATE_KEYS)]
        per_piece = len(tiles) // len(pieces)
        assert per_piece * len(pieces) == len(tiles)
        for n, (stage, cols) in enumerate(pieces):
            mine = tiles[n * per_piece:(n + 1) * per_piece]
            for lb, rg, kq in mine[:per_piece // 2]:
                gate_tile(at_cur, ga_new, c1_ref, p1_ref, row0, lb, rg, kq)
            stage(cols)
            for lb, rg, kq in mine[per_piece // 2:]:
                gate_tile(at_cur, ga_new, c1_ref, p1_ref, row0, lb, rg, kq)

    half_step(0, at0_ref, at1_ref, ga1_ref, ga0_ref, c1a_ref, p1a_ref, rows_half % SUBLANES)

    @pl.when(phase == 0)
    def _():
        load_keys()

    half_step(1, at1_ref, at0_ref, ga0_ref, ga1_ref, c1b_ref, p1b_ref, 0)

    @pl.when((phase == 0) & (j > 0))
    def _():
        is_sample = (j - 1) // PEER_STEPS_PER_BLOCK == T_PROMPT // PEER_TOKENS
        pe = jnp.where(is_sample, pes_ref[...], pep_ref[...])
        x2 = jnp.where(is_sample, x1s_ref[...], x1p_ref[...]) + acc_ref[...].T
        h3 = _rms(x2, ln3_ref[...]).astype(BF16)
        gate = _sigmoid(_dot(h3, wpg_ref[...]))
        x3 = x2 + gate * _dot(pe.astype(BF16), wple_ref[...])
        if final:
            y = _rms(x3, lnf_ref[...])

            @pl.when(jnp.logical_not(is_sample))
            def _():
                o_refs[0][...] = y

            @pl.when(is_sample)
            def _():
                o_refs[1][...] = y
        else:
            o_refs[0][...] = x3


def _peer(final, layer, h2t, r2, e2, c1, p1, u_pk, vt_pk, x1_p, x1_s, pe_p, pe_s, ln3, wpg, wple, lnf):
    nt = T_ALL // PEER_TOKENS
    n_prompt = T_PROMPT // PEER_TOKENS
    assert T_SAMPLE == PEER_TOKENS and nt == n_prompt + 1
    tiles_per_block = N_EXPERTS // HALF_EXPERTS
    n_tiles = nt * tiles_per_block
    assert tiles_per_block == 2 * PEER_STEPS_PER_BLOCK
    rows_half = HALF_EXPERTS // N_KEYS
    assert SUBLANES % rows_half == 0
    n_steps = n_tiles // 2 + 1
    lb_per_step = PEER_TOKENS // LANES

    prev_step = lambda j: jnp.maximum(j - 1, 0)
    tok_spec = lambda width: pl.BlockSpec(
        (PEER_TOKENS, width), lambda j: (prev_step(j) // PEER_STEPS_PER_BLOCK, 0))
    cur_block = lambda j: jnp.minimum(j // PEER_STEPS_PER_BLOCK, nt - 1)
    key_spec = pl.BlockSpec((HEADS, lb_per_step, N_KEYS, LANES), lambda j: (0, cur_block(j), 0, 0))

    def row_spec(tile_of_step):
        def index(j):
            s = jnp.clip(tile_of_step(j), 0, n_tiles - 1)
            return (0, s // tiles_per_block, (s % tiles_per_block) * rows_half // SUBLANES, 0, 0)
        return pl.BlockSpec((HEADS, lb_per_step, None, SUBLANES, LANES), index)

    rows_a = row_spec(lambda j: 2 * j - 1)
    rows_b = row_spec(lambda j: 2 * j)
    const = lambda shape: pl.BlockSpec(shape, lambda j: (0,) * len(shape), pipeline_mode=pl.Buffered(1))
    prompt_block = lambda j: jnp.minimum(prev_step(j) // PEER_STEPS_PER_BLOCK, n_prompt - 1)
    if final:
        out_specs = [pl.BlockSpec((PEER_TOKENS, D_MODEL), lambda j: (prompt_block(j), 0)),
                     pl.BlockSpec((PEER_TOKENS, D_MODEL), lambda j: (0, 0))]
        out_shape = [jax.ShapeDtypeStruct((T_PROMPT, D_MODEL), F32),
                     jax.ShapeDtypeStruct((T_SAMPLE, D_MODEL), F32)]
    else:
        out_specs = [tok_spec(D_MODEL)]
        out_shape = [jax.ShapeDtypeStruct((T_ALL, D_MODEL), F32)]
    return pl.pallas_call(
        functools.partial(_peer_kernel, final),
        grid=(n_steps,),
        in_specs=[
            pl.BlockSpec((D_MODEL // 2, PEER_TOKENS), lambda j: (0, cur_block(j))),
            key_spec, key_spec, rows_a, rows_a, rows_b, rows_b,
            pl.BlockSpec((None, HALF_EXPERTS, D_MODEL), lambda j: (layer, j % PEER_STEPS_PER_BLOCK, 0)),
            pl.BlockSpec((None, D_MODEL // 2, 2 * HALF_EXPERTS),
                         lambda j: (layer, 0, prev_step(j) % PEER_STEPS_PER_BLOCK)),
            pl.BlockSpec((PEER_TOKENS, D_MODEL), lambda j: (prompt_block(j), 0)),
            pl.BlockSpec((PEER_TOKENS, D_MODEL), lambda j: (0, 0), pipeline_mode=pl.Buffered(1)),
            pl.BlockSpec((None, PEER_TOKENS, D_PLE), lambda j: (layer, prompt_block(j), 0)),
            pl.BlockSpec((None, PEER_TOKENS, D_PLE), lambda j: (layer, 0, 0),
                         pipeline_mode=pl.Buffered(1)),
            const((1, D_MODEL)), const((D_MODEL, D_MODEL)), const((D_PLE, D_MODEL)), const((1, D_MODEL)),
        ],
        out_specs=out_specs,
        out_shape=out_shape,
        scratch_shapes=[pltpu.VMEM((D_MODEL, PEER_TOKENS), F32)]
        + [pltpu.VMEM((HEADS, lb_per_step, N_KEYS, LANES), BF16)] * 2
        + [pltpu.VMEM((HALF_EXPERTS, PEER_TOKENS), F32)] * 2
        + [pltpu.VMEM((HALF_EXPERTS, PEER_TOKENS), BF16)] * 2,
        compiler_params=pltpu.CompilerParams(
            dimension_semantics=("arbitrary",), vmem_limit_bytes=VMEM_LIMIT),
        name="peer_dense",
    )(h2t, r2, e2, c1, p1, c1, p1, u_pk, vt_pk, x1_p, x1_s, pe_p, pe_s, ln3, wpg, wple, lnf)


def _pack_tables_kernel(u_ref, v_ref, upk_ref, vtpk_ref):
    upk_ref[...] = pltpu.bitcast(u_ref[...].astype(BF16), jnp.uint32)
    vtpk_ref[...] = pltpu.bitcast(v_ref[...].T.astype(BF16), jnp.uint32)


def _pack_tables(u_tab, v_tab):
    nb = N_EXPERTS // PACK_EXPERTS
    in_spec = pl.BlockSpec((None, PACK_EXPERTS, D_MODEL), lambda l, e: (l, e, 0))
    return pl.pallas_call(
        _pack_tables_kernel,
        grid=(DEPTH, nb),
        in_specs=[in_spec, in_spec],
        out_specs=[
            pl.BlockSpec((None, PACK_EXPERTS // 2, D_MODEL), lambda l, e: (l, e, 0)),
            pl.BlockSpec((None, D_MODEL // 2, PACK_EXPERTS), lambda l, e: (l, 0, e)),
        ],
        out_shape=[
            jax.ShapeDtypeStruct((DEPTH, N_EXPERTS // 2, D_MODEL), jnp.uint32),
            jax.ShapeDtypeStruct((DEPTH, D_MODEL // 2, N_EXPERTS), jnp.uint32),
        ],
        compiler_params=pltpu.CompilerParams(
            dimension_semantics=("arbitrary", "arbitrary"), vmem_limit_bytes=VMEM_LIMIT),
        name="pack_tables",
    )(u_tab, v_tab)


def kernel(x_prompt, x_sample, state_conv, p_prompt, p_sample, ln1, w_in, ln_v, w_s, b_s, conv_w,
           w_a, w_b, w_o, ln2, w_q, k1, k2, u_tab, v_tab, ln3, w_pg, w_ple, ln_f):
    assert x_prompt.shape == (N_PROMPT_SEQ, PROMPT_LEN, D_MODEL)
    assert x_sample.shape == (N_SAMPLE_SEQ, SAMPLE_LEN, D_MODEL)
    assert u_tab.shape == (DEPTH, N_EXPERTS, D_MODEL)

    x_p = x_prompt.reshape(T_PROMPT, D_MODEL)
    x_s = x_sample.reshape(T_SAMPLE, D_MODEL)
    sample_block = 0
    seqs_per_chunk = CHUNK // SAMPLE_LEN
    eye = jnp.eye(seqs_per_chunk, dtype=F32)
    lnf = ln_f.reshape(1, D_MODEL)
    u_pk, vt_pk = _pack_tables(u_tab, v_tab)
    pe_p = p_prompt.reshape(DEPTH, T_PROMPT, D_PLE)
    pe_s = p_sample.reshape(DEPTH, T_SAMPLE, D_PLE)
    conv_prompt, conv_sample, v_sample = [], [], []
    y = None

    for i in range(DEPTH):
        row = lambda a: a[i].reshape(1, -1)
        ws_p = w_s[i].astype(BF16)
        corner = w_s[i][:, :SAMPLE_LEN, :SAMPLE_LEN]
        ws_s = (eye[None, :, None, :, None] * corner[:, None, :, None, :]).reshape(
            N_GROUPS, CHUNK, CHUNK).astype(BF16)
        bias_p = jnp.repeat(b_s[i].T, GROUP, axis=1)
        bias_s = jnp.tile(jnp.repeat(b_s[i][:, :SAMPLE_LEN].T, GROUP, axis=1), (seqs_per_chunk, 1))
        shared = (conv_w[i], w_a[i].astype(BF16), w_b[i].astype(BF16), w_o[i].astype(BF16))
        front = (row(ln1), w_in[i].astype(BF16), row(ln_v))
        w_prompt = front + (ws_p, bias_p) + shared
        w_sample = front + (ws_s, bias_s) + shared

        prev = state_conv[i]
        zero = jnp.zeros((N_SAMPLE_SEQ, 1, D_B), F32)
        tap1 = jnp.concatenate([prev[:, 1:2], zero, zero, zero], axis=1).reshape(T_SAMPLE, D_B)
        tap2 = jnp.concatenate([prev[:, 0:1], prev[:, 1:2], zero, zero], axis=1).reshape(T_SAMPLE, D_B)

        x1_p, tail_p = _mixer_prompt(x_p, w_prompt)
        x1_s, v_s, cx_s = _mixer_sample(x_s, sample_block, w_sample, tap1, tap2)
        conv_prompt.append(tail_p)
        conv_sample.append(cx_s.reshape(N_SAMPLE_SEQ, SAMPLE_LEN, D_B)[:, SAMPLE_LEN - 2:])
        v_sample.append(v_s.reshape(N_SAMPLE_SEQ, SAMPLE_LEN, D_A))

        h2t, r2, e2, c1, p1 = _select(x1_p, x1_s, row(ln2), w_q[i].T.astype(BF16),
                                      k1[i].astype(BF16), k2[i].astype(BF16))
        final = i == DEPTH - 1
        r2 = r2.reshape(HEADS, T_ALL // LANES, N_KEYS, LANES)
        e2 = e2.reshape(HEADS, T_ALL // LANES, N_KEYS, LANES)
        y = _peer(final, i, h2t, r2, e2, c1, p1, u_pk, vt_pk,
                  x1_p, x1_s, pe_p, pe_s, row(ln3), w_pg[i].astype(BF16), w_ple[i].astype(BF16), lnf)
        x_p = y[0]
        x_s = y[0]
        sample_block = T_PROMPT // T_SAMPLE

    y_prompt = y[0].reshape(N_PROMPT_SEQ, PROMPT_LEN, D_MODEL)
    y_sample = y[1].reshape(N_SAMPLE_SEQ, SAMPLE_LEN, D_MODEL)
    return (y_prompt, y_sample, jnp.stack(conv_prompt), jnp.stack(conv_sample), jnp.stack(v_sample))
```

```python
import functools

import jax
import jax.numpy as jnp
from jax import lax
from jax.experimental import pallas as pl
from jax.experimental.pallas import tpu as pltpu

F32 = jnp.float32
BF16 = jnp.bfloat16

D_MODEL = 1024
D_A = 1024
D_B = 1024
D_PLE = 256
CHUNK = 128
N_GROUPS = 8
GROUP = D_A // N_GROUPS
N_KEYS = 128
N_EXPERTS = N_KEYS * N_KEYS
HEADS = 8
D_QUERY = 256
D_HALF = 128
TOPK = 16
EPS = 1e-6
DEPTH = 2

N_PROMPT_SEQ = 8
PROMPT_LEN = 2048
N_SAMPLE_SEQ = 128
SAMPLE_LEN = 4
T_PROMPT = N_PROMPT_SEQ * PROMPT_LEN
T_SAMPLE = N_SAMPLE_SEQ * SAMPLE_LEN
T_ALL = T_PROMPT + T_SAMPLE

LANES = 128
SUBLANES = 8
BF16_ROWS = 16
MIX_ROWS = 512
SEL_TOKENS = 512
PEER_TOKENS = 512
HALF_EXPERTS = 1024
PEER_STEPS_PER_BLOCK = N_EXPERTS // (2 * HALF_EXPERTS)
GATE_ROWS = 4
REGIONS_PER_HALF = 1
GATE_KEYS = 64
PACK_EXPERTS = 1024
VMEM_LIMIT = 60 * 1024 * 1024
NEG = -1e30


def _rms(x, g):
    ms = jnp.mean(x * x, axis=-1, keepdims=True)
    return x * lax.rsqrt(ms + EPS) * g


def _sigmoid(x):
    return 1.0 / (1.0 + jnp.exp(-x))


def _dot(a, b):
    return jnp.dot(a, b, preferred_element_type=F32)


def _const_spec(shape, grid_rank):
    zeros = (0,) * len(shape)
    if grid_rank == 1:
        return pl.BlockSpec(shape, lambda i: zeros, pipeline_mode=pl.Buffered(1))
    return pl.BlockSpec(shape, lambda i, j: zeros, pipeline_mode=pl.Buffered(1))


def _mixer_front(x, ln1_ref, win_ref, lnv_ref, ws_ref, bias_ref, yin_ref, rows):
    h = _rms(x, ln1_ref[...]).astype(BF16)

    def proj(k):
        return _dot(h, win_ref[:, k * D_MODEL:(k + 1) * D_MODEL])

    u = jax.nn.gelu(proj(0))
    v = _rms(jax.nn.gelu(proj(1)), lnv_ref[...])
    vb = v.astype(BF16)
    r_i = lax.broadcasted_iota(jnp.int32, (CHUNK, CHUNK), 0)
    c_i = lax.broadcasted_iota(jnp.int32, (CHUNK, CHUNK), 1)
    causal = r_i >= c_i
    for g in range(N_GROUPS):
        w = jnp.where(causal, ws_ref[g], jnp.zeros((), BF16))
        cols = slice(g * GROUP, (g + 1) * GROUP)
        for c in range(rows // CHUNK):
            rws = slice(c * CHUNK, (c + 1) * CHUNK)
            s = _dot(w, vb[rws, cols]) + bias_ref[:, cols]
            yin_ref[rws, cols] = (u[rws, cols] * s).astype(BF16)
    return h, v, proj


def _mixer_back(x, proj, yc, wa_ref, wb_ref, wo_ref, yin_ref):
    ya = _dot(yin_ref[...], wa_ref[...])
    yb = _dot((proj(2) * yc).astype(BF16), wb_ref[...])
    mix = _sigmoid(proj(5)) * ya + _sigmoid(proj(6)) * yb
    return x + _dot(mix.astype(BF16), wo_ref[...])


def _mixer_prompt_kernel(x_ref, ln1_ref, win_ref, lnv_ref, ws_ref, bias_ref, cw_ref,
                         wa_ref, wb_ref, wo_ref, x1_ref, tail_ref, yin_ref, cbuf_ref):
    rows = MIX_ROWS
    x = x_ref[...]
    h, v, proj = _mixer_front(x, ln1_ref, win_ref, lnv_ref, ws_ref, bias_ref, yin_ref, rows)
    cx = proj(3) * proj(4)

    @pl.when(pl.program_id(1) == 0)
    def _():
        cbuf_ref[0:SUBLANES, :] = jnp.zeros((SUBLANES, D_B), F32)

    cbuf_ref[SUBLANES:SUBLANES + rows, :] = cx
    c1 = cbuf_ref[SUBLANES - 1:SUBLANES - 1 + rows, :]
    c2 = cbuf_ref[SUBLANES - 2:SUBLANES - 2 + rows, :]
    yc = cw_ref[0:1, :] * c2 + cw_ref[1:2, :] * c1 + cw_ref[2:3, :] * cx
    cbuf_ref[0:SUBLANES, :] = cx[rows - SUBLANES:rows, :]
    tail_ref[0] = cx[rows - 2:rows, :]
    x1_ref[...] = _mixer_back(x, proj, yc, wa_ref, wb_ref, wo_ref, yin_ref)


def _mixer_sample_kernel(x_ref, ln1_ref, win_ref, lnv_ref, ws_ref, bias_ref, cw_ref,
                         wa_ref, wb_ref, wo_ref, p1_ref, p2_ref,
                         x1_ref, v_ref, cx_ref, yin_ref, cbuf_ref):
    rows = T_SAMPLE
    x = x_ref[...]
    h, v, proj = _mixer_front(x, ln1_ref, win_ref, lnv_ref, ws_ref, bias_ref, yin_ref, rows)
    v_ref[...] = v
    cx = proj(3) * proj(4)
    cx_ref[...] = cx
    cbuf_ref[0:SUBLANES, :] = jnp.zeros((SUBLANES, D_B), F32)
    cbuf_ref[SUBLANES:SUBLANES + rows, :] = cx
    pos = lax.broadcasted_iota(jnp.int32, (rows, D_B), 0) & (SAMPLE_LEN - 1)
    c1 = jnp.where(pos >= 1, cbuf_ref[SUBLANES - 1:SUBLANES - 1 + rows, :], p1_ref[...])
    c2 = jnp.where(pos >= 2, cbuf_ref[SUBLANES - 2:SUBLANES - 2 + rows, :], p2_ref[...])
    yc = cw_ref[0:1, :] * c2 + cw_ref[1:2, :] * c1 + cw_ref[2:3, :] * cx
    x1_ref[...] = _mixer_back(x, proj, yc, wa_ref, wb_ref, wo_ref, yin_ref)


def _mixer_weight_specs(grid_rank):
    cs = functools.partial(_const_spec, grid_rank=grid_rank)
    return [
        cs((1, D_MODEL)),
        cs((D_MODEL, 7 * D_MODEL)),
        cs((1, D_A)),
        cs((N_GROUPS, CHUNK, CHUNK)),
        cs((CHUNK, D_A)),
        cs((3, D_B)),
        cs((D_A, D_MODEL)),
        cs((D_B, D_MODEL)),
        cs((D_MODEL, D_MODEL)),
    ]


def _mixer_prompt(x2d, weights):
    nj = PROMPT_LEN // MIX_ROWS
    return pl.pallas_call(
        _mixer_prompt_kernel,
        grid=(N_PROMPT_SEQ, nj),
        in_specs=[pl.BlockSpec((MIX_ROWS, D_MODEL), lambda b, j: (b * nj + j, 0))]
        + _mixer_weight_specs(2),
        out_specs=[
            pl.BlockSpec((MIX_ROWS, D_MODEL), lambda b, j: (b * nj + j, 0)),
            pl.BlockSpec((1, 2, D_B), lambda b, j: (b, 0, 0)),
        ],
        out_shape=[
            jax.ShapeDtypeStruct((T_PROMPT, D_MODEL), F32),
            jax.ShapeDtypeStruct((N_PROMPT_SEQ, 2, D_B), F32),
        ],
        scratch_shapes=[
            pltpu.VMEM((MIX_ROWS, D_A), BF16),
            pltpu.VMEM((MIX_ROWS + SUBLANES, D_B), F32),
        ],
        compiler_params=pltpu.CompilerParams(
            dimension_semantics=("arbitrary", "arbitrary"), vmem_limit_bytes=VMEM_LIMIT),
        name="mixer_prompt",
    )(x2d, *weights)


def _mixer_sample(x2d, row_block, weights, p1, p2):
    full = lambda shape: pl.BlockSpec(shape, lambda i: (0,) * len(shape))
    return pl.pallas_call(
        _mixer_sample_kernel,
        grid=(1,),
        in_specs=[pl.BlockSpec((T_SAMPLE, D_MODEL), lambda i: (row_block, 0))]
        + _mixer_weight_specs(1)
        + [full((T_SAMPLE, D_B)), full((T_SAMPLE, D_B))],
        out_specs=[full((T_SAMPLE, D_MODEL)), full((T_SAMPLE, D_A)), full((T_SAMPLE, D_B))],
        out_shape=[
            jax.ShapeDtypeStruct((T_SAMPLE, D_MODEL), F32),
            jax.ShapeDtypeStruct((T_SAMPLE, D_A), F32),
            jax.ShapeDtypeStruct((T_SAMPLE, D_B), F32),
        ],
        scratch_shapes=[
            pltpu.VMEM((T_SAMPLE, D_A), BF16),
            pltpu.VMEM((T_SAMPLE + SUBLANES, D_B), F32),
        ],
        compiler_params=pltpu.CompilerParams(
            dimension_semantics=("arbitrary",), vmem_limit_bytes=VMEM_LIMIT),
        name="mixer_sample",
    )(x2d, *weights, p1, p2)


def _cmpx(v, i, j):
    hi = jnp.maximum(v[i], v[j])
    lo = jnp.minimum(v[i], v[j])
    v[i] = hi
    v[j] = lo


def _bitonic_merge16(v):
    j = TOPK // 2
    while j >= 1:
        for i in range(TOPK):
            l = i ^ j
            if l > i:
                _cmpx(v, i, l)
        j //= 2


def _sort16(v):
    k = 2
    while k <= TOPK:
        j = k // 2
        while j >= 1:
            for i in range(TOPK):
                l = i ^ j
                if l > i:
                    if (i & k) == 0:
                        _cmpx(v, i, l)
                    else:
                        _cmpx(v, l, i)
            j //= 2
        k *= 2


def _top16(s):
    v = [s[k] for k in range(TOPK)]
    _sort16(v)
    for shift in (4, 2, 1):
        w = [pltpu.roll(v[k], shift, 0) for k in range(TOPK)]
        v = [jnp.maximum(v[k], w[TOPK - 1 - k]) for k in range(TOPK)]
        _bitonic_merge16(v)
    return v


def _sublane_allreduce(x, op):
    for shift in (4, 2, 1):
        x = op(x, pltpu.roll(x, shift, 0))
    return x


def _on_sublanes(vals, sub):
    out = vals[SUBLANES - 1]
    for r in range(SUBLANES - 2, -1, -1):
        out = jnp.where(sub == r, vals[r], out)
    return out


def _bf16_row_pairs(x):
    return pltpu.bitcast(x.reshape(N_KEYS, LANES).astype(BF16), jnp.uint32)


def _count_above(t, x, above):
    bits = []
    for step in range(4):
        width = TOPK >> (step + 1)
        pivots = [t[base + width - 1] for base in range(0, TOPK, 2 * width)]
        for bit in reversed(bits):
            pivots = [jnp.where(bit, hi, lo) for lo, hi in zip(pivots[0::2], pivots[1::2])]
        pivot = pivots[0]
        bits.append(above(pivot[None] if pivot.ndim == 2 else pivot, x))
    count = jnp.where(above(t[TOPK - 1][None], x), 1.0, 0.0)
    for step, bit in enumerate(bits):
        count = count + jnp.where(bit, float(TOPK >> (step + 1)), 0.0)
    return count


def _select_kernel(x1p_ref, x1s_ref, ln2_ref, wqt_ref, k1_ref, k2_ref,
                   h2t_ref, r2_ref, e2_ref, c1_ref, p1_ref, s1_scr, s2_scr):
    tokens = SEL_TOKENS
    is_sample = pl.program_id(0) == T_PROMPT // SEL_TOKENS
    h2 = _rms(jnp.where(is_sample, x1s_ref[...], x1p_ref[...]), ln2_ref[...])
    h2t = h2.T.astype(BF16)
    h2t_ref[...] = pltpu.bitcast(h2t, jnp.uint32)
    qt = _dot(wqt_ref[...], h2t)
    for h in range(HEADS):
        q1 = qt[h * D_QUERY:h * D_QUERY + D_HALF].astype(BF16)
        q2 = qt[h * D_QUERY + D_HALF:(h + 1) * D_QUERY].astype(BF16)
        s1_scr[h] = _dot(k1_ref[...], q1).reshape(TOPK, SUBLANES, tokens)
        s2_scr[h] = _dot(k2_ref[...], q2).reshape(TOPK, SUBLANES, tokens)

    sub = lax.broadcasted_iota(jnp.int32, (SUBLANES, LANES), 0)

    def lane_block(lb, carry):
        lanes = pl.ds(pl.multiple_of(lb * LANES, LANES), LANES)
        for h in range(HEADS):
            s1 = s1_scr[h, :, :, lanes]
            s2 = s2_scr[h, :, :, lanes]
            t1 = _top16(s1)
            t2 = _top16(s2)
            t2lo = _on_sublanes(t2[0:8], sub)
            t2hi = _on_sublanes(t2[8:16], sub)
            t1hi = _on_sublanes(t1[8:16], sub)
            cands = [t1[0] + t2lo, t1[0] + t2hi, t1[1] + t2lo]
            for a, nb in ((2, 5), (3, 4), (4, 3), (5, 2), (6, 2), (7, 2)):
                cands.append(jnp.where(sub < nb, t1[a] + t2lo, NEG))
            cands.append(t1hi + t2[0])
            cur = list(cands)
            tops = []
            for it in range(TOPK + 1):
                m = functools.reduce(jnp.maximum, cur)
                m = _sublane_allreduce(m, jnp.maximum)
                tops.append(m)
                if it < TOPK:
                    cur = [jnp.where(c == m, NEG, c) for c in cur]
            tau = 0.5 * (tops[TOPK - 1] + tops[TOPK])
            z = functools.reduce(
                jnp.add, [jnp.where(c >= tau, jnp.exp(c - tops[0]), 0.0) for c in cands])
            z = _sublane_allreduce(z, jnp.add)
            inv_z = 1.0 / z
            p1 = jnp.where(s1 >= t1[TOPK - 1][None], jnp.exp(s1 - t1[0][None]) * inv_z[None], 0.0)
            e2 = jnp.where(s2 >= t2[TOPK - 1][None], jnp.exp(s2 - t2[0][None]), 0.0)
            th = tau[None] - s1
            rank2 = _count_above(t2, s2, jnp.greater)
            r2_ref[h, lb] = _bf16_row_pairs(rank2)
            e2_ref[h, lb] = _bf16_row_pairs(e2)
            c1_ref[h, lb] = _count_above(t2, th, jnp.greater_equal)
            p1_ref[h, lb] = p1
        return carry

    lax.fori_loop(0, tokens // LANES, lane_block, 0)


def _select(x1_p, x1_s, ln2, wqt, k1, k2):
    nb = T_ALL // SEL_TOKENS
    n_prompt = T_PROMPT // SEL_TOKENS
    assert T_SAMPLE == SEL_TOKENS and nb == n_prompt + 1
    lb_per_step = SEL_TOKENS // LANES
    n_lb = T_ALL // LANES
    key_shape = jax.ShapeDtypeStruct((HEADS, n_lb, TOPK, SUBLANES, LANES), F32)
    key_spec = pl.BlockSpec((HEADS, lb_per_step, TOPK, SUBLANES, LANES), lambda i: (0, i, 0, 0, 0))
    pair_shape = jax.ShapeDtypeStruct((HEADS, n_lb, N_KEYS // 2, LANES), jnp.uint32)
    pair_spec = pl.BlockSpec((HEADS, lb_per_step, N_KEYS // 2, LANES), lambda i: (0, i, 0, 0))
    scr = pltpu.VMEM((HEADS, TOPK, SUBLANES, SEL_TOKENS), F32)
    return pl.pallas_call(
        _select_kernel,
        grid=(nb,),
        in_specs=[
            pl.BlockSpec((SEL_TOKENS, D_MODEL), lambda i: (jnp.minimum(i, n_prompt - 1), 0)),
            pl.BlockSpec((SEL_TOKENS, D_MODEL), lambda i: (0, 0)),
            _const_spec((1, D_MODEL), 1),
            _const_spec((HEADS * D_QUERY, D_MODEL), 1),
            _const_spec((N_KEYS, D_HALF), 1),
            _const_spec((N_KEYS, D_HALF), 1),
        ],
        out_specs=[pl.BlockSpec((D_MODEL // 2, SEL_TOKENS), lambda i: (0, i)),
                   pair_spec, pair_spec, key_spec, key_spec],
        out_shape=[jax.ShapeDtypeStruct((D_MODEL // 2, T_ALL), jnp.uint32),
                   pair_shape, pair_shape, key_shape, key_shape],
        scratch_shapes=[scr, scr],
        compiler_params=pltpu.CompilerParams(
            dimension_semantics=("arbitrary",), vmem_limit_bytes=VMEM_LIMIT),
        name="peer_select",
    )(x1_p, x1_s, ln2, wqt, k1, k2)


def _peer_kernel(final, h2t_ref, r2a_ref, e2a_ref, r2b_ref, e2b_ref, c1a_ref, p1a_ref, c1b_ref, p1b_ref,
                 u_ref, vt_ref,
                 x1p_ref, x1s_ref, pep_ref, pes_ref, ln3_ref, wpg_ref, wple_ref, lnf_ref, *out_and_scratch):
    n_out = 2 if final else 1
    o_refs = out_and_scratch[:n_out]
    acc_ref, at0_ref, at1_ref, ga0_ref, ga1_ref = out_and_scratch[n_out:]
    _peer_body(final, h2t_ref, r2a_ref, e2a_ref, r2b_ref, e2b_ref, c1a_ref, p1a_ref, c1b_ref, p1b_ref,
               u_ref, vt_ref,
               x1p_ref, x1s_ref, pep_ref, pes_ref, ln3_ref, wpg_ref, wple_ref, lnf_ref, o_refs,
               acc_ref, at0_ref, at1_ref, ga0_ref, ga1_ref)


def _peer_body(final, h2t_ref, r2a_ref, e2a_ref, r2b_ref, e2b_ref, c1a_ref, p1a_ref, c1b_ref, p1b_ref,
               u_ref, vt_ref,
               x1p_ref, x1s_ref, pep_ref, pes_ref, ln3_ref, wpg_ref, wple_ref, lnf_ref, o_refs,
               acc_ref, at0_ref, at1_ref, ga0_ref, ga1_ref):
    j = pl.program_id(0)
    phase = j % PEER_STEPS_PER_BLOCK
    rows_half = HALF_EXPERTS // N_KEYS
    vregs = GATE_KEYS // BF16_ROWS

    @pl.when(j == 0)
    def _():
        at1_ref[...] = jnp.zeros_like(at1_ref)
        ga0_ref[...] = jnp.zeros_like(ga0_ref)
        ga1_ref[...] = jnp.zeros_like(ga1_ref)

    @pl.when((phase == 1) | (j == 0))
    def _():
        acc_ref[...] = jnp.zeros_like(acc_ref)

    def lane_bcast(ref, h, lb, r):
        return jnp.broadcast_to(ref[h, lb, r:r + 1, :], (BF16_ROWS, LANES)).astype(BF16)

    def gate_tile(at_ref, ga_ref, r2_ref, e2_ref, c1_ref, p1_ref, row0, lb, rg, kq):
        lanes = slice(lb * LANES, (lb + 1) * LANES)
        keys = [slice(kq * GATE_KEYS + k * BF16_ROWS, kq * GATE_KEYS + (k + 1) * BF16_ROWS)
                for k in range(vregs)]
        key_rows = range(rg * GATE_ROWS, (rg + 1) * GATE_ROWS)
        g = {r: [None] * vregs for r in key_rows}
        for h in range(HEADS):
            r2 = [pltpu.bitcast(r2_ref[h, lb, ks.start // 2:ks.stop // 2, :], BF16) for ks in keys]
            e2 = [pltpu.bitcast(e2_ref[h, lb, ks.start // 2:ks.stop // 2, :], BF16) for ks in keys]
            for r in key_rows:
                c = lane_bcast(c1_ref, h, lb, row0 + r)
                p = lane_bcast(p1_ref, h, lb, row0 + r)
                for k in range(vregs):
                    term = jnp.where(r2[k] < c, e2[k], jnp.zeros_like(e2[k])) * p
                    g[r][k] = term if h == 0 else g[r][k] + term
        for r in key_rows:
            for k in range(vregs):
                rows = slice(r * N_KEYS + keys[k].start, r * N_KEYS + keys[k].stop)
                a = at_ref[rows, lanes].astype(BF16)
                ga_ref[rows, lanes] = g[r][k] * jax.nn.gelu(a)

    def half_step(half, at_new, at_cur, ga_new, ga_old, r2_ref, e2_ref, c1_ref, p1_ref, row0):
        u_rows = slice(half * HALF_EXPERTS // 2, (half + 1) * HALF_EXPERTS // 2)
        v_cols = slice(half * HALF_EXPERTS, (half + 1) * HALF_EXPERTS)
        def project(cols):
            at_new[:, cols] = _dot(pltpu.bitcast(u_ref[u_rows, :], BF16),
                                   pltpu.bitcast(h2t_ref[:, cols], BF16))

        def accumulate(cols):
            acc_ref[:, cols] += _dot(pltpu.bitcast(vt_ref[:, v_cols], BF16), ga_old[:, cols])

        halves = [slice(n * PEER_TOKENS // 2, (n + 1) * PEER_TOKENS // 2) for n in range(2)]
        pieces = [(stage, cols) for stage in (project, accumulate) for cols in halves]
        tiles = [(lb, rg, kq) for lb in range(PEER_TOKENS // LANES)
                 for rg in range(rows_half // GATE_ROWS) for kq in range(N_KEYS // GATE_KEYS)]
        per_piece = len(tiles) // len(pieces)
        assert per_piece * len(pieces) == len(tiles)

        def run_piece(n):
            stage, cols = pieces[n]
            mine = tiles[n * per_piece:(n + 1) * per_piece]
            for lb, rg, kq in mine[:per_piece // 2]:
                gate_tile(at_cur, ga_new, r2_ref, e2_ref, c1_ref, p1_ref, row0, lb, rg, kq)
            stage(cols)
            for lb, rg, kq in mine[per_piece // 2:]:
                gate_tile(at_cur, ga_new, r2_ref, e2_ref, c1_ref, p1_ref, row0, lb, rg, kq)

        per_region = len(pieces) // REGIONS_PER_HALF
        for region in range(REGIONS_PER_HALF):
            @pl.when(j > -1 - region - REGIONS_PER_HALF * half)
            def _(region=region):
                for n in range(region * per_region, (region + 1) * per_region):
                    run_piece(n)

    half_step(0, at0_ref, at1_ref, ga1_ref, ga0_ref, r2a_ref, e2a_ref, c1a_ref, p1a_ref,
              rows_half % SUBLANES)

    half_step(1, at1_ref, at0_ref, ga0_ref, ga1_ref, r2b_ref, e2b_ref, c1b_ref, p1b_ref, 0)

    @pl.when((phase == 0) & (j > 0))
    def _():
        is_sample = (j - 1) // PEER_STEPS_PER_BLOCK == T_PROMPT // PEER_TOKENS
        pe = jnp.where(is_sample, pes_ref[...], pep_ref[...])
        x2 = jnp.where(is_sample, x1s_ref[...], x1p_ref[...]) + acc_ref[...].T
        h3 = _rms(x2, ln3_ref[...]).astype(BF16)
        gate = _sigmoid(_dot(h3, wpg_ref[...]))
        x3 = x2 + gate * _dot(pe.astype(BF16), wple_ref[...])
        if final:
            y = _rms(x3, lnf_ref[...])

            @pl.when(jnp.logical_not(is_sample))
            def _():
                o_refs[0][...] = y

            @pl.when(is_sample)
            def _():
                o_refs[1][...] = y
        else:
            o_refs[0][...] = x3


def _peer(final, layer, h2t, r2, e2, c1, p1, u_pk, vt_pk, x1_p, x1_s, pe_p, pe_s, ln3, wpg, wple, lnf):
    nt = T_ALL // PEER_TOKENS
    n_prompt = T_PROMPT // PEER_TOKENS
    assert T_SAMPLE == PEER_TOKENS and nt == n_prompt + 1
    tiles_per_block = N_EXPERTS // HALF_EXPERTS
    n_tiles = nt * tiles_per_block
    assert tiles_per_block == 2 * PEER_STEPS_PER_BLOCK
    rows_half = HALF_EXPERTS // N_KEYS
    assert SUBLANES % rows_half == 0
    n_steps = n_tiles // 2 + 1
    lb_per_step = PEER_TOKENS // LANES

    prev_step = lambda j: jnp.maximum(j - 1, 0)
    tok_spec = lambda width: pl.BlockSpec(
        (PEER_TOKENS, width), lambda j: (prev_step(j) // PEER_STEPS_PER_BLOCK, 0))
    cur_block = lambda j: jnp.minimum(j // PEER_STEPS_PER_BLOCK, nt - 1)

    def gate_specs(tile_of_step):
        tile = lambda j: jnp.clip(tile_of_step(j), 0, n_tiles - 1)
        pairs = pl.BlockSpec((HEADS, lb_per_step, N_KEYS // 2, LANES),
                             lambda j: (0, tile(j) // tiles_per_block, 0, 0))
        rows = pl.BlockSpec(
            (HEADS, lb_per_step, None, SUBLANES, LANES),
            lambda j: (0, tile(j) // tiles_per_block,
                       (tile(j) % tiles_per_block) * rows_half // SUBLANES, 0, 0))
        return pairs, rows

    pairs_a, rows_a = gate_specs(lambda j: 2 * j - 1)
    pairs_b, rows_b = gate_specs(lambda j: 2 * j)
    const = lambda shape: pl.BlockSpec(shape, lambda j: (0,) * len(shape), pipeline_mode=pl.Buffered(1))
    prompt_block = lambda j: jnp.minimum(prev_step(j) // PEER_STEPS_PER_BLOCK, n_prompt - 1)
    if final:
        out_specs = [pl.BlockSpec((PEER_TOKENS, D_MODEL), lambda j: (prompt_block(j), 0)),
                     pl.BlockSpec((PEER_TOKENS, D_MODEL), lambda j: (0, 0))]
        out_shape = [jax.ShapeDtypeStruct((T_PROMPT, D_MODEL), F32),
                     jax.ShapeDtypeStruct((T_SAMPLE, D_MODEL), F32)]
    else:
        out_specs = [tok_spec(D_MODEL)]
        out_shape = [jax.ShapeDtypeStruct((T_ALL, D_MODEL), F32)]
    return pl.pallas_call(
        functools.partial(_peer_kernel, final),
        grid=(n_steps,),
        in_specs=[
            pl.BlockSpec((D_MODEL // 2, PEER_TOKENS), lambda j: (0, cur_block(j))),
            pairs_a, pairs_a, pairs_b, pairs_b, rows_a, rows_a, rows_b, rows_b,
            pl.BlockSpec((None, HALF_EXPERTS, D_MODEL), lambda j: (layer, j % PEER_STEPS_PER_BLOCK, 0)),
            pl.BlockSpec((None, D_MODEL // 2, 2 * HALF_EXPERTS),
                         lambda j: (layer, 0, prev_step(j) % PEER_STEPS_PER_BLOCK)),
            pl.BlockSpec((PEER_TOKENS, D_MODEL), lambda j: (prompt_block(j), 0)),
            pl.BlockSpec((PEER_TOKENS, D_MODEL), lambda j: (0, 0), pipeline_mode=pl.Buffered(1)),
            pl.BlockSpec((None, PEER_TOKENS, D_PLE), lambda j: (layer, prompt_block(j), 0)),
            pl.BlockSpec((None, PEER_TOKENS, D_PLE), lambda j: (layer, 0, 0),
                         pipeline_mode=pl.Buffered(1)),
            const((1, D_MODEL)), const((D_MODEL, D_MODEL)), const((D_PLE, D_MODEL)), const((1, D_MODEL)),
        ],
        out_specs=out_specs,
        out_shape=out_shape,
        scratch_shapes=[pltpu.VMEM((D_MODEL, PEER_TOKENS), F32)]
        + [pltpu.VMEM((HALF_EXPERTS, PEER_TOKENS), F32)] * 2
        + [pltpu.VMEM((HALF_EXPERTS, PEER_TOKENS), BF16)] * 2,
        compiler_params=pltpu.CompilerParams(
            dimension_semantics=("arbitrary",), vmem_limit_bytes=VMEM_LIMIT),
        name="peer_dense",
    )(h2t, r2, e2, r2, e2, c1, p1, c1, p1, u_pk, vt_pk, x1_p, x1_s, pe_p, pe_s, ln3, wpg, wple, lnf)


def _pack_tables_kernel(u_ref, v_ref, upk_ref, vtpk_ref):
    upk_ref[...] = pltpu.bitcast(u_ref[...].astype(BF16), jnp.uint32)
    vtpk_ref[...] = pltpu.bitcast(v_ref[...].T.astype(BF16), jnp.uint32)


def _pack_tables(u_tab, v_tab):
    nb = N_EXPERTS // PACK_EXPERTS
    in_spec = pl.BlockSpec((None, PACK_EXPERTS, D_MODEL), lambda l, e: (l, e, 0))
    return pl.pallas_call(
        _pack_tables_kernel,
        grid=(DEPTH, nb),
        in_specs=[in_spec, in_spec],
        out_specs=[
            pl.BlockSpec((None, PACK_EXPERTS // 2, D_MODEL), lambda l, e: (l, e, 0)),
            pl.BlockSpec((None, D_MODEL // 2, PACK_EXPERTS), lambda l, e: (l, 0, e)),
        ],
        out_shape=[
            jax.ShapeDtypeStruct((DEPTH, N_EXPERTS // 2, D_MODEL), jnp.uint32),
            jax.ShapeDtypeStruct((DEPTH, D_MODEL // 2, N_EXPERTS), jnp.uint32),
        ],
        compiler_params=pltpu.CompilerParams(
            dimension_semantics=("arbitrary", "arbitrary"), vmem_limit_bytes=VMEM_LIMIT),
        name="pack_tables",
    )(u_tab, v_tab)


def kernel(x_prompt, x_sample, state_conv, p_prompt, p_sample, ln1, w_in, ln_v, w_s, b_s, conv_w,
           w_a, w_b, w_o, ln2, w_q, k1, k2, u_tab, v_tab, ln3, w_pg, w_ple, ln_f):
    assert x_prompt.shape == (N_PROMPT_SEQ, PROMPT_LEN, D_MODEL)
    assert x_sample.shape == (N_SAMPLE_SEQ, SAMPLE_LEN, D_MODEL)
    assert u_tab.shape == (DEPTH, N_EXPERTS, D_MODEL)

    x_p = x_prompt.reshape(T_PROMPT, D_MODEL)
    x_s = x_sample.reshape(T_SAMPLE, D_MODEL)
    sample_block = 0
    seqs_per_chunk = CHUNK // SAMPLE_LEN
    eye = jnp.eye(seqs_per_chunk, dtype=F32)
    lnf = ln_f.reshape(1, D_MODEL)
    u_pk, vt_pk = _pack_tables(u_tab, v_tab)
    pe_p = p_prompt.reshape(DEPTH, T_PROMPT, D_PLE)
    pe_s = p_sample.reshape(DEPTH, T_SAMPLE, D_PLE)
    conv_prompt, conv_sample, v_sample = [], [], []
    y = None

    for i in range(DEPTH):
        row = lambda a: a[i].reshape(1, -1)
        ws_p = w_s[i].astype(BF16)
        corner = w_s[i][:, :SAMPLE_LEN, :SAMPLE_LEN]
        ws_s = (eye[None, :, None, :, None] * corner[:, None, :, None, :]).reshape(
            N_GROUPS, CHUNK, CHUNK).astype(BF16)
        bias_p = jnp.repeat(b_s[i].T, GROUP, axis=1)
        bias_s = jnp.tile(jnp.repeat(b_s[i][:, :SAMPLE_LEN].T, GROUP, axis=1), (seqs_per_chunk, 1))
        shared = (conv_w[i], w_a[i].astype(BF16), w_b[i].astype(BF16), w_o[i].astype(BF16))
        front = (row(ln1), w_in[i].astype(BF16), row(ln_v))
        w_prompt = front + (ws_p, bias_p) + shared
        w_sample = front + (ws_s, bias_s) + shared

        prev = state_conv[i]
        zero = jnp.zeros((N_SAMPLE_SEQ, 1, D_B), F32)
        tap1 = jnp.concatenate([prev[:, 1:2], zero, zero, zero], axis=1).reshape(T_SAMPLE, D_B)
        tap2 = jnp.concatenate([prev[:, 0:1], prev[:, 1:2], zero, zero], axis=1).reshape(T_SAMPLE, D_B)

        x1_p, tail_p = _mixer_prompt(x_p, w_prompt)
        x1_s, v_s, cx_s = _mixer_sample(x_s, sample_block, w_sample, tap1, tap2)
        conv_prompt.append(tail_p)
        conv_sample.append(cx_s.reshape(N_SAMPLE_SEQ, SAMPLE_LEN, D_B)[:, SAMPLE_LEN - 2:])
        v_sample.append(v_s.reshape(N_SAMPLE_SEQ, SAMPLE_LEN, D_A))

        h2t, r2, e2, c1, p1 = _select(x1_p, x1_s, row(ln2), w_q[i].T.astype(BF16),
                                      k1[i].astype(BF16), k2[i].astype(BF16))
        final = i == DEPTH - 1
        y = _peer(final, i, h2t, r2, e2, c1, p1, u_pk, vt_pk,
                  x1_p, x1_s, pe_p, pe_s, row(ln3), w_pg[i].astype(BF16), w_ple[i].astype(BF16), lnf)
        x_p = y[0]
        x_s = y[0]
        sample_block = T_PROMPT // T_SAMPLE

    y_prompt = y[0].reshape(N_PROMPT_SEQ, PROMPT_LEN, D_MODEL)
    y_sample = y[1].reshape(N_SAMPLE_SEQ, SAMPLE_LEN, D_MODEL)
    return (y_prompt, y_sample, jnp.stack(conv_prompt), jnp.stack(conv_sample), jnp.stack(v_sample))
```

```python
import functools

import jax
import jax.numpy as jnp
from jax import lax
from jax.experimental import pallas as pl
from jax.experimental.pallas import tpu as pltpu

F32 = jnp.float32
BF16 = jnp.bfloat16

D_MODEL = 1024
D_A = 1024
D_B = 1024
D_PLE = 256
CHUNK = 128
N_GROUPS = 8
GROUP = D_A // N_GROUPS
N_KEYS = 128
N_EXPERTS = N_KEYS * N_KEYS
HEADS = 8
D_QUERY = 256
D_HALF = 128
TOPK = 16
EPS = 1e-6
DEPTH = 2

N_PROMPT_SEQ = 8
PROMPT_LEN = 2048
N_SAMPLE_SEQ = 128
SAMPLE_LEN = 4
T_PROMPT = N_PROMPT_SEQ * PROMPT_LEN
T_SAMPLE = N_SAMPLE_SEQ * SAMPLE_LEN
T_ALL = T_PROMPT + T_SAMPLE

LANES = 128
SUBLANES = 8
BF16_ROWS = 16
MIX_ROWS = 512
SEL_TOKENS = 512
PEER_TOKENS = 512
HALF_EXPERTS = 1024
PEER_STEPS_PER_BLOCK = N_EXPERTS // (2 * HALF_EXPERTS)
GATE_ROWS = 4
GATE_KEYS = 64
PACK_EXPERTS = 1024
VMEM_LIMIT = 60 * 1024 * 1024
NEG = -1e30


def _rms(x, g):
    ms = jnp.mean(x * x, axis=-1, keepdims=True)
    return x * lax.rsqrt(ms + EPS) * g


def _sigmoid(x):
    return 1.0 / (1.0 + jnp.exp(-x))


def _dot(a, b):
    return jnp.dot(a, b, preferred_element_type=F32)


def _const_spec(shape, grid_rank):
    zeros = (0,) * len(shape)
    if grid_rank == 1:
        return pl.BlockSpec(shape, lambda i: zeros, pipeline_mode=pl.Buffered(1))
    return pl.BlockSpec(shape, lambda i, j: zeros, pipeline_mode=pl.Buffered(1))


def _mixer_front(x, ln1_ref, win_ref, lnv_ref, ws_ref, bias_ref, yin_ref, rows):
    h = _rms(x, ln1_ref[...]).astype(BF16)

    def proj(k):
        return _dot(h, win_ref[:, k * D_MODEL:(k + 1) * D_MODEL])

    u = jax.nn.gelu(proj(0))
    v = _rms(jax.nn.gelu(proj(1)), lnv_ref[...])
    vb = v.astype(BF16)
    r_i = lax.broadcasted_iota(jnp.int32, (CHUNK, CHUNK), 0)
    c_i = lax.broadcasted_iota(jnp.int32, (CHUNK, CHUNK), 1)
    causal = r_i >= c_i
    for g in range(N_GROUPS):
        w = jnp.where(causal, ws_ref[g], jnp.zeros((), BF16))
        cols = slice(g * GROUP, (g + 1) * GROUP)
        for c in range(rows // CHUNK):
            rws = slice(c * CHUNK, (c + 1) * CHUNK)
            s = _dot(w, vb[rws, cols]) + bias_ref[:, cols]
            yin_ref[rws, cols] = (u[rws, cols] * s).astype(BF16)
    return h, v, proj


def _mixer_back(x, proj, yc, wa_ref, wb_ref, wo_ref, yin_ref):
    ya = _dot(yin_ref[...], wa_ref[...])
    yb = _dot((proj(2) * yc).astype(BF16), wb_ref[...])
    mix = _sigmoid(proj(5)) * ya + _sigmoid(proj(6)) * yb
    return x + _dot(mix.astype(BF16), wo_ref[...])


def _mixer_prompt_kernel(x_ref, ln1_ref, win_ref, lnv_ref, ws_ref, bias_ref, cw_ref,
                         wa_ref, wb_ref, wo_ref, x1_ref, tail_ref, yin_ref, cbuf_ref):
    rows = MIX_ROWS
    x = x_ref[...]
    h, v, proj = _mixer_front(x, ln1_ref, win_ref, lnv_ref, ws_ref, bias_ref, yin_ref, rows)
    cx = proj(3) * proj(4)

    @pl.when(pl.program_id(1) == 0)
    def _():
        cbuf_ref[0:SUBLANES, :] = jnp.zeros((SUBLANES, D_B), F32)

    cbuf_ref[SUBLANES:SUBLANES + rows, :] = cx
    c1 = cbuf_ref[SUBLANES - 1:SUBLANES - 1 + rows, :]
    c2 = cbuf_ref[SUBLANES - 2:SUBLANES - 2 + rows, :]
    yc = cw_ref[0:1, :] * c2 + cw_ref[1:2, :] * c1 + cw_ref[2:3, :] * cx
    cbuf_ref[0:SUBLANES, :] = cx[rows - SUBLANES:rows, :]
    tail_ref[0] = cx[rows - 2:rows, :]
    x1_ref[...] = _mixer_back(x, proj, yc, wa_ref, wb_ref, wo_ref, yin_ref)


def _mixer_sample_kernel(x_ref, ln1_ref, win_ref, lnv_ref, ws_ref, bias_ref, cw_ref,
                         wa_ref, wb_ref, wo_ref, p1_ref, p2_ref,
                         x1_ref, v_ref, cx_ref, yin_ref, cbuf_ref):
    rows = T_SAMPLE
    x = x_ref[...]
    h, v, proj = _mixer_front(x, ln1_ref, win_ref, lnv_ref, ws_ref, bias_ref, yin_ref, rows)
    v_ref[...] = v
    cx = proj(3) * proj(4)
    cx_ref[...] = cx
    cbuf_ref[0:SUBLANES, :] = jnp.zeros((SUBLANES, D_B), F32)
    cbuf_ref[SUBLANES:SUBLANES + rows, :] = cx
    pos = lax.broadcasted_iota(jnp.int32, (rows, D_B), 0) & (SAMPLE_LEN - 1)
    c1 = jnp.where(pos >= 1, cbuf_ref[SUBLANES - 1:SUBLANES - 1 + rows, :], p1_ref[...])
    c2 = jnp.where(pos >= 2, cbuf_ref[SUBLANES - 2:SUBLANES - 2 + rows, :], p2_ref[...])
    yc = cw_ref[0:1, :] * c2 + cw_ref[1:2, :] * c1 + cw_ref[2:3, :] * cx
    x1_ref[...] = _mixer_back(x, proj, yc, wa_ref, wb_ref, wo_ref, yin_ref)


def _mixer_weight_specs(grid_rank):
    cs = functools.partial(_const_spec, grid_rank=grid_rank)
    return [
        cs((1, D_MODEL)),
        cs((D_MODEL, 7 * D_MODEL)),
        cs((1, D_A)),
        cs((N_GROUPS, CHUNK, CHUNK)),
        cs((CHUNK, D_A)),
        cs((3, D_B)),
        cs((D_A, D_MODEL)),
        cs((D_B, D_MODEL)),
        cs((D_MODEL, D_MODEL)),
    ]


def _mixer_prompt(x2d, weights):
    nj = PROMPT_LEN // MIX_ROWS
    return pl.pallas_call(
        _mixer_prompt_kernel,
        grid=(N_PROMPT_SEQ, nj),
        in_specs=[pl.BlockSpec((MIX_ROWS, D_MODEL), lambda b, j: (b * nj + j, 0))]
        + _mixer_weight_specs(2),
        out_specs=[
            pl.BlockSpec((MIX_ROWS, D_MODEL), lambda b, j: (b * nj + j, 0)),
            pl.BlockSpec((1, 2, D_B), lambda b, j: (b, 0, 0)),
        ],
        out_shape=[
            jax.ShapeDtypeStruct((T_PROMPT, D_MODEL), F32),
            jax.ShapeDtypeStruct((N_PROMPT_SEQ, 2, D_B), F32),
        ],
        scratch_shapes=[
            pltpu.VMEM((MIX_ROWS, D_A), BF16),
            pltpu.VMEM((MIX_ROWS + SUBLANES, D_B), F32),
        ],
        compiler_params=pltpu.CompilerParams(
            dimension_semantics=("arbitrary", "arbitrary"), vmem_limit_bytes=VMEM_LIMIT),
        name="mixer_prompt",
    )(x2d, *weights)


def _mixer_sample(x2d, row_block, weights, p1, p2):
    full = lambda shape: pl.BlockSpec(shape, lambda i: (0,) * len(shape))
    return pl.pallas_call(
        _mixer_sample_kernel,
        grid=(1,),
        in_specs=[pl.BlockSpec((T_SAMPLE, D_MODEL), lambda i: (row_block, 0))]
        + _mixer_weight_specs(1)
        + [full((T_SAMPLE, D_B)), full((T_SAMPLE, D_B))],
        out_specs=[full((T_SAMPLE, D_MODEL)), full((T_SAMPLE, D_A)), full((T_SAMPLE, D_B))],
        out_shape=[
            jax.ShapeDtypeStruct((T_SAMPLE, D_MODEL), F32),
            jax.ShapeDtypeStruct((T_SAMPLE, D_A), F32),
            jax.ShapeDtypeStruct((T_SAMPLE, D_B), F32),
        ],
        scratch_shapes=[
            pltpu.VMEM((T_SAMPLE, D_A), BF16),
            pltpu.VMEM((T_SAMPLE + SUBLANES, D_B), F32),
        ],
        compiler_params=pltpu.CompilerParams(
            dimension_semantics=("arbitrary",), vmem_limit_bytes=VMEM_LIMIT),
        name="mixer_sample",
    )(x2d, *weights, p1, p2)


def _cmpx(v, i, j):
    hi = jnp.maximum(v[i], v[j])
    lo = jnp.minimum(v[i], v[j])
    v[i] = hi
    v[j] = lo


def _bitonic_merge16(v):
    j = TOPK // 2
    while j >= 1:
        for i in range(TOPK):
            l = i ^ j
            if l > i:
                _cmpx(v, i, l)
        j //= 2


def _sort16(v):
    k = 2
    while k <= TOPK:
        j = k // 2
        while j >= 1:
            for i in range(TOPK):
                l = i ^ j
                if l > i:
                    if (i & k) == 0:
                        _cmpx(v, i, l)
                    else:
                        _cmpx(v, l, i)
            j //= 2
        k *= 2


def _top16(s):
    v = [s[k] for k in range(TOPK)]
    _sort16(v)
    for shift in (4, 2, 1):
        w = [pltpu.roll(v[k], shift, 0) for k in range(TOPK)]
        v = [jnp.maximum(v[k], w[TOPK - 1 - k]) for k in range(TOPK)]
        _bitonic_merge16(v)
    return v


def _sublane_allreduce(x, op):
    for shift in (4, 2, 1):
        x = op(x, pltpu.roll(x, shift, 0))
    return x


def _on_sublanes(vals, sub):
    out = vals[SUBLANES - 1]
    for r in range(SUBLANES - 2, -1, -1):
        out = jnp.where(sub == r, vals[r], out)
    return out


def _bf16_row_pairs(x):
    return pltpu.bitcast(x.reshape(N_KEYS, LANES).astype(BF16), jnp.uint32)


def _count_above(t, x, above):
    bits = []
    for step in range(4):
        width = TOPK >> (step + 1)
        pivots = [t[base + width - 1] for base in range(0, TOPK, 2 * width)]
        for bit in reversed(bits):
            pivots = [jnp.where(bit, hi, lo) for lo, hi in zip(pivots[0::2], pivots[1::2])]
        pivot = pivots[0]
        bits.append(above(pivot[None] if pivot.ndim == 2 else pivot, x))
    count = jnp.where(above(t[TOPK - 1][None], x), 1.0, 0.0)
    for step, bit in enumerate(bits):
        count = count + jnp.where(bit, float(TOPK >> (step + 1)), 0.0)
    return count


def _select_kernel(x1p_ref, x1s_ref, ln2_ref, wqt_ref, k1_ref, k2_ref,
                   h2t_ref, r2_ref, e2_ref, c1_ref, p1_ref, s1_scr, s2_scr):
    tokens = SEL_TOKENS
    is_sample = pl.program_id(0) == T_PROMPT // SEL_TOKENS
    h2 = _rms(jnp.where(is_sample, x1s_ref[...], x1p_ref[...]), ln2_ref[...])
    h2t = h2.T.astype(BF16)
    h2t_ref[...] = pltpu.bitcast(h2t, jnp.uint32)
    qt = _dot(wqt_ref[...], h2t)
    for h in range(HEADS):
        q1 = qt[h * D_QUERY:h * D_QUERY + D_HALF].astype(BF16)
        q2 = qt[h * D_QUERY + D_HALF:(h + 1) * D_QUERY].astype(BF16)
        s1_scr[h] = _dot(k1_ref[...], q1).reshape(TOPK, SUBLANES, tokens)
        s2_scr[h] = _dot(k2_ref[...], q2).reshape(TOPK, SUBLANES, tokens)

    sub = lax.broadcasted_iota(jnp.int32, (SUBLANES, LANES), 0)

    def lane_block(lb, carry):
        lanes = pl.ds(pl.multiple_of(lb * LANES, LANES), LANES)
        for h in range(HEADS):
            s1 = s1_scr[h, :, :, lanes]
            s2 = s2_scr[h, :, :, lanes]
            t1 = _top16(s1)
            t2 = _top16(s2)
            t2lo = _on_sublanes(t2[0:8], sub)
            t2hi = _on_sublanes(t2[8:16], sub)
            t1hi = _on_sublanes(t1[8:16], sub)
            cands = [t1[0] + t2lo, t1[0] + t2hi, t1[1] + t2lo]
            for a, nb in ((2, 5), (3, 4), (4, 3), (5, 2), (6, 2), (7, 2)):
                cands.append(jnp.where(sub < nb, t1[a] + t2lo, NEG))
            cands.append(t1hi + t2[0])
            cur = list(cands)
            tops = []
            for it in range(TOPK + 1):
                m = functools.reduce(jnp.maximum, cur)
                m = _sublane_allreduce(m, jnp.maximum)
                tops.append(m)
                if it < TOPK:
                    cur = [jnp.where(c == m, NEG, c) for c in cur]
            tau = 0.5 * (tops[TOPK - 1] + tops[TOPK])
            z = functools.reduce(
                jnp.add, [jnp.where(c >= tau, jnp.exp(c - tops[0]), 0.0) for c in cands])
            z = _sublane_allreduce(z, jnp.add)
            inv_z = 1.0 / z
            p1 = jnp.where(s1 >= t1[TOPK - 1][None], jnp.exp(s1 - t1[0][None]) * inv_z[None], 0.0)
            e2 = jnp.where(s2 >= t2[TOPK - 1][None], jnp.exp(s2 - t2[0][None]), 0.0)
            th = tau[None] - s1
            rank2 = _count_above(t2, s2, jnp.greater)
            r2_ref[h, lb] = _bf16_row_pairs(rank2)
            e2_ref[h, lb] = _bf16_row_pairs(e2)
            c1_ref[h, lb] = _count_above(t2, th, jnp.greater_equal)
            p1_ref[h, lb] = p1
        return carry

    lax.fori_loop(0, tokens // LANES, lane_block, 0)


def _select(x1_p, x1_s, ln2, wqt, k1, k2):
    nb = T_ALL // SEL_TOKENS
    n_prompt = T_PROMPT // SEL_TOKENS
    assert T_SAMPLE == SEL_TOKENS and nb == n_prompt + 1
    lb_per_step = SEL_TOKENS // LANES
    n_lb = T_ALL // LANES
    key_shape = jax.ShapeDtypeStruct((HEADS, n_lb, TOPK, SUBLANES, LANES), F32)
    key_spec = pl.BlockSpec((HEADS, lb_per_step, TOPK, SUBLANES, LANES), lambda i: (0, i, 0, 0, 0))
    pair_shape = jax.ShapeDtypeStruct((HEADS, n_lb, N_KEYS // 2, LANES), jnp.uint32)
    pair_spec = pl.BlockSpec((HEADS, lb_per_step, N_KEYS // 2, LANES), lambda i: (0, i, 0, 0))
    scr = pltpu.VMEM((HEADS, TOPK, SUBLANES, SEL_TOKENS), F32)
    return pl.pallas_call(
        _select_kernel,
        grid=(nb,),
        in_specs=[
            pl.BlockSpec((SEL_TOKENS, D_MODEL), lambda i: (jnp.minimum(i, n_prompt - 1), 0)),
            pl.BlockSpec((SEL_TOKENS, D_MODEL), lambda i: (0, 0)),
            _const_spec((1, D_MODEL), 1),
            _const_spec((HEADS * D_QUERY, D_MODEL), 1),
            _const_spec((N_KEYS, D_HALF), 1),
            _const_spec((N_KEYS, D_HALF), 1),
        ],
        out_specs=[pl.BlockSpec((D_MODEL // 2, SEL_TOKENS), lambda i: (0, i)),
                   pair_spec, pair_spec, key_spec, key_spec],
        out_shape=[jax.ShapeDtypeStruct((D_MODEL // 2, T_ALL), jnp.uint32),
                   pair_shape, pair_shape, key_shape, key_shape],
        scratch_shapes=[scr, scr],
        compiler_params=pltpu.CompilerParams(
            dimension_semantics=("arbitrary",), vmem_limit_bytes=VMEM_LIMIT),
        name="peer_select",
    )(x1_p, x1_s, ln2, wqt, k1, k2)


def _peer_kernel(final, h2t_ref, r2a_ref, e2a_ref, r2b_ref, e2b_ref, c1a_ref, p1a_ref, c1b_ref, p1b_ref,
                 u_ref, vt_ref,
                 x1p_ref, x1s_ref, pep_ref, pes_ref, ln3_ref, wpg_ref, wple_ref, lnf_ref, *out_and_scratch):
    n_out = 2 if final else 1
    o_refs = out_and_scratch[:n_out]
    acc_ref, at0_ref, at1_ref, ga0_ref, ga1_ref = out_and_scratch[n_out:]
    _peer_body(final, h2t_ref, r2a_ref, e2a_ref, r2b_ref, e2b_ref, c1a_ref, p1a_ref, c1b_ref, p1b_ref,
               u_ref, vt_ref,
               x1p_ref, x1s_ref, pep_ref, pes_ref, ln3_ref, wpg_ref, wple_ref, lnf_ref, o_refs,
               acc_ref, at0_ref, at1_ref, ga0_ref, ga1_ref)


def _peer_body(final, h2t_ref, r2a_ref, e2a_ref, r2b_ref, e2b_ref, c1a_ref, p1a_ref, c1b_ref, p1b_ref,
               u_ref, vt_ref,
               x1p_ref, x1s_ref, pep_ref, pes_ref, ln3_ref, wpg_ref, wple_ref, lnf_ref, o_refs,
               acc_ref, at0_ref, at1_ref, ga0_ref, ga1_ref):
    j = pl.program_id(0)
    phase = j % PEER_STEPS_PER_BLOCK
    rows_half = HALF_EXPERTS // N_KEYS
    vregs = GATE_KEYS // BF16_ROWS

    @pl.when(j == 0)
    def _():
        at1_ref[...] = jnp.zeros_like(at1_ref)
        ga0_ref[...] = jnp.zeros_like(ga0_ref)
        ga1_ref[...] = jnp.zeros_like(ga1_ref)

    @pl.when((phase == 1) | (j == 0))
    def _():
        acc_ref[...] = jnp.zeros_like(acc_ref)

    def lane_bcast(ref, h, lb, r):
        return jnp.broadcast_to(ref[h, lb, r:r + 1, :], (BF16_ROWS, LANES)).astype(BF16)

    def gate_tile(at_ref, ga_ref, r2_ref, e2_ref, c1_ref, p1_ref, row0, lb, rg, kq):
        lanes = slice(lb * LANES, (lb + 1) * LANES)
        keys = [slice(kq * GATE_KEYS + k * BF16_ROWS, kq * GATE_KEYS + (k + 1) * BF16_ROWS)
                for k in range(vregs)]
        key_rows = range(rg * GATE_ROWS, (rg + 1) * GATE_ROWS)
        g = {r: [None] * vregs for r in key_rows}
        for h in range(HEADS):
            r2 = [pltpu.bitcast(r2_ref[h, lb, ks.start // 2:ks.stop // 2, :], BF16) for ks in keys]
            e2 = [pltpu.bitcast(e2_ref[h, lb, ks.start // 2:ks.stop // 2, :], BF16) for ks in keys]
            for r in key_rows:
                c = lane_bcast(c1_ref, h, lb, row0 + r)
                p = lane_bcast(p1_ref, h, lb, row0 + r)
                for k in range(vregs):
                    term = jnp.where(r2[k] < c, e2[k], jnp.zeros_like(e2[k])) * p
                    g[r][k] = term if h == 0 else g[r][k] + term
        for r in key_rows:
            for k in range(vregs):
                rows = slice(r * N_KEYS + keys[k].start, r * N_KEYS + keys[k].stop)
                a = at_ref[rows, lanes].astype(BF16)
                ga_ref[rows, lanes] = g[r][k] * jax.nn.gelu(a)

    def half_step(half, at_new, at_cur, ga_new, ga_old, r2_ref, e2_ref, c1_ref, p1_ref, row0):
        u_rows = slice(half * HALF_EXPERTS // 2, (half + 1) * HALF_EXPERTS // 2)
        v_cols = slice(half * HALF_EXPERTS, (half + 1) * HALF_EXPERTS)
        def project(cols):
            at_new[:, cols] = _dot(pltpu.bitcast(u_ref[u_rows, :], BF16),
                                   pltpu.bitcast(h2t_ref[:, cols], BF16))

        def accumulate(cols):
            acc_ref[:, cols] += _dot(pltpu.bitcast(vt_ref[:, v_cols], BF16), ga_old[:, cols])

        halves = [slice(n * PEER_TOKENS // 2, (n + 1) * PEER_TOKENS // 2) for n in range(2)]
        pieces = [(stage, cols) for stage in (project, accumulate) for cols in halves]
        tiles = [(lb, rg, kq) for lb in range(PEER_TOKENS // LANES)
                 for rg in range(rows_half // GATE_ROWS) for kq in range(N_KEYS // GATE_KEYS)]
        per_piece = len(tiles) // len(pieces)
        assert per_piece * len(pieces) == len(tiles)

        @pl.when(j > -1 - half)
        def _():
            for n, (stage, cols) in enumerate(pieces):
                mine = tiles[n * per_piece:(n + 1) * per_piece]
                for lb, rg, kq in mine[:per_piece // 2]:
                    gate_tile(at_cur, ga_new, r2_ref, e2_ref, c1_ref, p1_ref, row0, lb, rg, kq)
                stage(cols)
                for lb, rg, kq in mine[per_piece // 2:]:
                    gate_tile(at_cur, ga_new, r2_ref, e2_ref, c1_ref, p1_ref, row0, lb, rg, kq)

    half_step(0, at0_ref, at1_ref, ga1_ref, ga0_ref, r2a_ref, e2a_ref, c1a_ref, p1a_ref,
              rows_half % SUBLANES)

    half_step(1, at1_ref, at0_ref, ga0_ref, ga1_ref, r2b_ref, e2b_ref, c1b_ref, p1b_ref, 0)

    @pl.when((phase == 0) & (j > 0))
    def _():
        is_sample = (j - 1) // PEER_STEPS_PER_BLOCK == T_PROMPT // PEER_TOKENS
        pe = jnp.where(is_sample, pes_ref[...], pep_ref[...])
        x2 = jnp.where(is_sample, x1s_ref[...], x1p_ref[...]) + acc_ref[...].T
        h3 = _rms(x2, ln3_ref[...]).astype(BF16)
        gate = _sigmoid(_dot(h3, wpg_ref[...]))
        x3 = x2 + gate * _dot(pe.astype(BF16), wple_ref[...])
        if final:
            y = _rms(x3, lnf_ref[...])

            @pl.when(jnp.logical_not(is_sample))
            def _():
                o_refs[0][...] = y

            @pl.when(is_sample)
            def _():
                o_refs[1][...] = y
        else:
            o_refs[0][...] = x3


def _peer(final, layer, h2t, r2, e2, c1, p1, u_pk, vt_pk, x1_p, x1_s, pe_p, pe_s, ln3, wpg, wple, lnf):
    nt = T_ALL // PEER_TOKENS
    n_prompt = T_PROMPT // PEER_TOKENS
    assert T_SAMPLE == PEER_TOKENS and nt == n_prompt + 1
    tiles_per_block = N_EXPERTS // HALF_EXPERTS
    n_tiles = nt * tiles_per_block
    assert tiles_per_block == 2 * PEER_STEPS_PER_BLOCK
    rows_half = HALF_EXPERTS // N_KEYS
    assert SUBLANES % rows_half == 0
    n_steps = n_tiles // 2 + 1
    lb_per_step = PEER_TOKENS // LANES

    prev_step = lambda j: jnp.maximum(j - 1, 0)
    tok_spec = lambda width: pl.BlockSpec(
        (PEER_TOKENS, width), lambda j: (prev_step(j) // PEER_STEPS_PER_BLOCK, 0))
    cur_block = lambda j: jnp.minimum(j // PEER_STEPS_PER_BLOCK, nt - 1)

    def gate_specs(tile_of_step):
        tile = lambda j: jnp.clip(tile_of_step(j), 0, n_tiles - 1)
        pairs = pl.BlockSpec((HEADS, lb_per_step, N_KEYS // 2, LANES),
                             lambda j: (0, tile(j) // tiles_per_block, 0, 0))
        rows = pl.BlockSpec(
            (HEADS, lb_per_step, None, SUBLANES, LANES),
            lambda j: (0, tile(j) // tiles_per_block,
                       (tile(j) % tiles_per_block) * rows_half // SUBLANES, 0, 0))
        return pairs, rows

    pairs_a, rows_a = gate_specs(lambda j: 2 * j - 1)
    pairs_b, rows_b = gate_specs(lambda j: 2 * j)
    const = lambda shape: pl.BlockSpec(shape, lambda j: (0,) * len(shape), pipeline_mode=pl.Buffered(1))
    prompt_block = lambda j: jnp.minimum(prev_step(j) // PEER_STEPS_PER_BLOCK, n_prompt - 1)
    if final:
        out_specs = [pl.BlockSpec((PEER_TOKENS, D_MODEL), lambda j: (prompt_block(j), 0)),
                     pl.BlockSpec((PEER_TOKENS, D_MODEL), lambda j: (0, 0))]
        out_shape = [jax.ShapeDtypeStruct((T_PROMPT, D_MODEL), F32),
                     jax.ShapeDtypeStruct((T_SAMPLE, D_MODEL), F32)]
    else:
        out_specs = [tok_spec(D_MODEL)]
        out_shape = [jax.ShapeDtypeStruct((T_ALL, D_MODEL), F32)]
    return pl.pallas_call(
        functools.partial(_peer_kernel, final),
        grid=(n_steps,),
        in_specs=[
            pl.BlockSpec((D_MODEL // 2, PEER_TOKENS), lambda j: (0, cur_block(j))),
            pairs_a, pairs_a, pairs_b, pairs_b, rows_a, rows_a, rows_b, rows_b,
            pl.BlockSpec((None, HALF_EXPERTS, D_MODEL), lambda j: (layer, j % PEER_STEPS_PER_BLOCK, 0)),
            pl.BlockSpec((None, D_MODEL // 2, 2 * HALF_EXPERTS),
                         lambda j: (layer, 0, prev_step(j) % PEER_STEPS_PER_BLOCK)),
            pl.BlockSpec((PEER_TOKENS, D_MODEL), lambda j: (prompt_block(j), 0)),
            pl.BlockSpec((PEER_TOKENS, D_MODEL), lambda j: (0, 0), pipeline_mode=pl.Buffered(1)),
            pl.BlockSpec((None, PEER_TOKENS, D_PLE), lambda j: (layer, prompt_block(j), 0)),
            pl.BlockSpec((None, PEER_TOKENS, D_PLE), lambda j: (layer, 0, 0),
                         pipeline_mode=pl.Buffered(1)),
            const((1, D_MODEL)), const((D_MODEL, D_MODEL)), const((D_PLE, D_MODEL)), const((1, D_MODEL)),
        ],
        out_specs=out_specs,
        out_shape=out_shape,
        scratch_shapes=[pltpu.VMEM((D_MODEL, PEER_TOKENS), F32)]
        + [pltpu.VMEM((HALF_EXPERTS, PEER_TOKENS), F32)] * 2
        + [pltpu.VMEM((HALF_EXPERTS, PEER_TOKENS), BF16)] * 2,
        compiler_params=pltpu.CompilerParams(
            dimension_semantics=("arbitrary",), vmem_limit_bytes=VMEM_LIMIT),
        name="peer_dense",
    )(h2t, r2, e2, r2, e2, c1, p1, c1, p1, u_pk, vt_pk, x1_p, x1_s, pe_p, pe_s, ln3, wpg, wple, lnf)


def _pack_tables_kernel(u_ref, v_ref, upk_ref, vtpk_ref):
    upk_ref[...] = pltpu.bitcast(u_ref[...].astype(BF16), jnp.uint32)
    vtpk_ref[...] = pltpu.bitcast(v_ref[...].T.astype(BF16), jnp.uint32)


def _pack_tables(u_tab, v_tab):
    nb = N_EXPERTS // PACK_EXPERTS
    in_spec = pl.BlockSpec((None, PACK_EXPERTS, D_MODEL), lambda l, e: (l, e, 0))
    return pl.pallas_call(
        _pack_tables_kernel,
        grid=(DEPTH, nb),
        in_specs=[in_spec, in_spec],
        out_specs=[
            pl.BlockSpec((None, PACK_EXPERTS // 2, D_MODEL), lambda l, e: (l, e, 0)),
            pl.BlockSpec((None, D_MODEL // 2, PACK_EXPERTS), lambda l, e: (l, 0, e)),
        ],
        out_shape=[
            jax.ShapeDtypeStruct((DEPTH, N_EXPERTS // 2, D_MODEL), jnp.uint32),
            jax.ShapeDtypeStruct((DEPTH, D_MODEL // 2, N_EXPERTS), jnp.uint32),
        ],
        compiler_params=pltpu.CompilerParams(
            dimension_semantics=("arbitrary", "arbitrary"), vmem_limit_bytes=VMEM_LIMIT),
        name="pack_tables",
    )(u_tab, v_tab)


def kernel(x_prompt, x_sample, state_conv, p_prompt, p_sample, ln1, w_in, ln_v, w_s, b_s, conv_w,
           w_a, w_b, w_o, ln2, w_q, k1, k2, u_tab, v_tab, ln3, w_pg, w_ple, ln_f):
    assert x_prompt.shape == (N_PROMPT_SEQ, PROMPT_LEN, D_MODEL)
    assert x_sample.shape == (N_SAMPLE_SEQ, SAMPLE_LEN, D_MODEL)
    assert u_tab.shape == (DEPTH, N_EXPERTS, D_MODEL)

    x_p = x_prompt.reshape(T_PROMPT, D_MODEL)
    x_s = x_sample.reshape(T_SAMPLE, D_MODEL)
    sample_block = 0
    seqs_per_chunk = CHUNK // SAMPLE_LEN
    eye = jnp.eye(seqs_per_chunk, dtype=F32)
    lnf = ln_f.reshape(1, D_MODEL)
    u_pk, vt_pk = _pack_tables(u_tab, v_tab)
    pe_p = p_prompt.reshape(DEPTH, T_PROMPT, D_PLE)
    pe_s = p_sample.reshape(DEPTH, T_SAMPLE, D_PLE)
    conv_prompt, conv_sample, v_sample = [], [], []
    y = None

    for i in range(DEPTH):
        row = lambda a: a[i].reshape(1, -1)
        ws_p = w_s[i].astype(BF16)
        corner = w_s[i][:, :SAMPLE_LEN, :SAMPLE_LEN]
        ws_s = (eye[None, :, None, :, None] * corner[:, None, :, None, :]).reshape(
            N_GROUPS, CHUNK, CHUNK).astype(BF16)
        bias_p = jnp.repeat(b_s[i].T, GROUP, axis=1)
        bias_s = jnp.tile(jnp.repeat(b_s[i][:, :SAMPLE_LEN].T, GROUP, axis=1), (seqs_per_chunk, 1))
        shared = (conv_w[i], w_a[i].astype(BF16), w_b[i].astype(BF16), w_o[i].astype(BF16))
        front = (row(ln1), w_in[i].astype(BF16), row(ln_v))
        w_prompt = front + (ws_p, bias_p) + shared
        w_sample = front + (ws_s, bias_s) + shared

        prev = state_conv[i]
        zero = jnp.zeros((N_SAMPLE_SEQ, 1, D_B), F32)
        tap1 = jnp.concatenate([prev[:, 1:2], zero, zero, zero], axis=1).reshape(T_SAMPLE, D_B)
        tap2 = jnp.concatenate([prev[:, 0:1], prev[:, 1:2], zero, zero], axis=1).reshape(T_SAMPLE, D_B)

        x1_p, tail_p = _mixer_prompt(x_p, w_prompt)
        x1_s, v_s, cx_s = _mixer_sample(x_s, sample_block, w_sample, tap1, tap2)
        conv_prompt.append(tail_p)
        conv_sample.append(cx_s.reshape(N_SAMPLE_SEQ, SAMPLE_LEN, D_B)[:, SAMPLE_LEN - 2:])
        v_sample.append(v_s.reshape(N_SAMPLE_SEQ, SAMPLE_LEN, D_A))

        h2t, r2, e2, c1, p1 = _select(x1_p, x1_s, row(ln2), w_q[i].T.astype(BF16),
                                      k1[i].astype(BF16), k2[i].astype(BF16))
        final = i == DEPTH - 1
        y = _peer(final, i, h2t, r2, e2, c1, p1, u_pk, vt_pk,
                  x1_p, x1_s, pe_p, pe_s, row(ln3), w_pg[i].astype(BF16), w_ple[i].astype(BF16), lnf)
        x_p = y[0]
        x_s = y[0]
        sample_block = T_PROMPT // T_SAMPLE

    y_prompt = y[0].reshape(N_PROMPT_SEQ, PROMPT_LEN, D_MODEL)
    y_sample = y[1].reshape(N_SAMPLE_SEQ, SAMPLE_LEN, D_MODEL)
    return (y_prompt, y_sample, jnp.stack(conv_prompt), jnp.stack(conv_sample), jnp.stack(v_sample))
```

```python
import functools

import jax
import jax.numpy as jnp
from jax import lax
from jax.experimental import pallas as pl
from jax.experimental.pallas import tpu as pltpu

F32 = jnp.float32
BF16 = jnp.bfloat16

D_MODEL = 1024
D_A = 1024
D_B = 1024
D_PLE = 256
CHUNK = 128
N_GROUPS = 8
GROUP = D_A // N_GROUPS
N_KEYS = 128
N_EXPERTS = N_KEYS * N_KEYS
HEADS = 8
D_QUERY = 256
D_HALF = 128
TOPK = 16
EPS = 1e-6
DEPTH = 2

N_PROMPT_SEQ = 8
PROMPT_LEN = 2048
N_SAMPLE_SEQ = 128
SAMPLE_LEN = 4
T_PROMPT = N_PROMPT_SEQ * PROMPT_LEN
T_SAMPLE = N_SAMPLE_SEQ * SAMPLE_LEN
T_ALL = T_PROMPT + T_SAMPLE

LANES = 128
SUBLANES = 8
BF16_ROWS = 16
MIX_ROWS = 512
SEL_TOKENS = 512
PEER_TOKENS = 512
HALF_EXPERTS = 1024
PEER_STEPS_PER_BLOCK = N_EXPERTS // (2 * HALF_EXPERTS)
GATE_ROWS = 4
GATE_KEYS = 64
PACK_EXPERTS = 1024
VMEM_LIMIT = 60 * 1024 * 1024
NEG = -1e30


def _rms(x, g):
    ms = jnp.mean(x * x, axis=-1, keepdims=True)
    return x * lax.rsqrt(ms + EPS) * g


def _sigmoid(x):
    return 1.0 / (1.0 + jnp.exp(-x))


def _dot(a, b):
    return jnp.dot(a, b, preferred_element_type=F32)


def _const_spec(shape, grid_rank):
    zeros = (0,) * len(shape)
    if grid_rank == 1:
        return pl.BlockSpec(shape, lambda i: zeros, pipeline_mode=pl.Buffered(1))
    return pl.BlockSpec(shape, lambda i, j: zeros, pipeline_mode=pl.Buffered(1))


def _mixer_front(x, ln1_ref, win_ref, lnv_ref, ws_ref, bias_ref, yin_ref, rows):
    h = _rms(x, ln1_ref[...]).astype(BF16)

    def proj(k):
        return _dot(h, win_ref[:, k * D_MODEL:(k + 1) * D_MODEL])

    u = jax.nn.gelu(proj(0))
    v = _rms(jax.nn.gelu(proj(1)), lnv_ref[...])
    vb = v.astype(BF16)
    r_i = lax.broadcasted_iota(jnp.int32, (CHUNK, CHUNK), 0)
    c_i = lax.broadcasted_iota(jnp.int32, (CHUNK, CHUNK), 1)
    causal = r_i >= c_i
    for g in range(N_GROUPS):
        w = jnp.where(causal, ws_ref[g], jnp.zeros((), BF16))
        cols = slice(g * GROUP, (g + 1) * GROUP)
        for c in range(rows // CHUNK):
            rws = slice(c * CHUNK, (c + 1) * CHUNK)
            s = _dot(w, vb[rws, cols]) + bias_ref[:, cols]
            yin_ref[rws, cols] = (u[rws, cols] * s).astype(BF16)
    return h, v, proj


def _mixer_back(x, proj, yc, wa_ref, wb_ref, wo_ref, yin_ref):
    ya = _dot(yin_ref[...], wa_ref[...])
    yb = _dot((proj(2) * yc).astype(BF16), wb_ref[...])
    mix = _sigmoid(proj(5)) * ya + _sigmoid(proj(6)) * yb
    return x + _dot(mix.astype(BF16), wo_ref[...])


def _mixer_prompt_kernel(x_ref, ln1_ref, win_ref, lnv_ref, ws_ref, bias_ref, cw_ref,
                         wa_ref, wb_ref, wo_ref, x1_ref, tail_ref, yin_ref, cbuf_ref):
    rows = MIX_ROWS
    x = x_ref[...]
    h, v, proj = _mixer_front(x, ln1_ref, win_ref, lnv_ref, ws_ref, bias_ref, yin_ref, rows)
    cx = proj(3) * proj(4)

    @pl.when(pl.program_id(1) == 0)
    def _():
        cbuf_ref[0:SUBLANES, :] = jnp.zeros((SUBLANES, D_B), F32)

    cbuf_ref[SUBLANES:SUBLANES + rows, :] = cx
    c1 = cbuf_ref[SUBLANES - 1:SUBLANES - 1 + rows, :]
    c2 = cbuf_ref[SUBLANES - 2:SUBLANES - 2 + rows, :]
    yc = cw_ref[0:1, :] * c2 + cw_ref[1:2, :] * c1 + cw_ref[2:3, :] * cx
    cbuf_ref[0:SUBLANES, :] = cx[rows - SUBLANES:rows, :]
    tail_ref[0] = cx[rows - 2:rows, :]
    x1_ref[...] = _mixer_back(x, proj, yc, wa_ref, wb_ref, wo_ref, yin_ref)


def _mixer_sample_kernel(x_ref, ln1_ref, win_ref, lnv_ref, ws_ref, bias_ref, cw_ref,
                         wa_ref, wb_ref, wo_ref, p1_ref, p2_ref,
                         x1_ref, v_ref, cx_ref, yin_ref, cbuf_ref):
    rows = T_SAMPLE
    x = x_ref[...]
    h, v, proj = _mixer_front(x, ln1_ref, win_ref, lnv_ref, ws_ref, bias_ref, yin_ref, rows)
    v_ref[...] = v
    cx = proj(3) * proj(4)
    cx_ref[...] = cx
    cbuf_ref[0:SUBLANES, :] = jnp.zeros((SUBLANES, D_B), F32)
    cbuf_ref[SUBLANES:SUBLANES + rows, :] = cx
    pos = lax.broadcasted_iota(jnp.int32, (rows, D_B), 0) & (SAMPLE_LEN - 1)
    c1 = jnp.where(pos >= 1, cbuf_ref[SUBLANES - 1:SUBLANES - 1 + rows, :], p1_ref[...])
    c2 = jnp.where(pos >= 2, cbuf_ref[SUBLANES - 2:SUBLANES - 2 + rows, :], p2_ref[...])
    yc = cw_ref[0:1, :] * c2 + cw_ref[1:2, :] * c1 + cw_ref[2:3, :] * cx
    x1_ref[...] = _mixer_back(x, proj, yc, wa_ref, wb_ref, wo_ref, yin_ref)


def _mixer_weight_specs(grid_rank):
    cs = functools.partial(_const_spec, grid_rank=grid_rank)
    return [
        cs((1, D_MODEL)),
        cs((D_MODEL, 7 * D_MODEL)),
        cs((1, D_A)),
        cs((N_GROUPS, CHUNK, CHUNK)),
        cs((CHUNK, D_A)),
        cs((3, D_B)),
        cs((D_A, D_MODEL)),
        cs((D_B, D_MODEL)),
        cs((D_MODEL, D_MODEL)),
    ]


def _mixer_prompt(x2d, weights):
    nj = PROMPT_LEN // MIX_ROWS
    return pl.pallas_call(
        _mixer_prompt_kernel,
        grid=(N_PROMPT_SEQ, nj),
        in_specs=[pl.BlockSpec((MIX_ROWS, D_MODEL), lambda b, j: (b * nj + j, 0))]
        + _mixer_weight_specs(2),
        out_specs=[
            pl.BlockSpec((MIX_ROWS, D_MODEL), lambda b, j: (b * nj + j, 0)),
            pl.BlockSpec((1, 2, D_B), lambda b, j: (b, 0, 0)),
        ],
        out_shape=[
            jax.ShapeDtypeStruct((T_PROMPT, D_MODEL), F32),
            jax.ShapeDtypeStruct((N_PROMPT_SEQ, 2, D_B), F32),
        ],
        scratch_shapes=[
            pltpu.VMEM((MIX_ROWS, D_A), BF16),
            pltpu.VMEM((MIX_ROWS + SUBLANES, D_B), F32),
        ],
        compiler_params=pltpu.CompilerParams(
            dimension_semantics=("arbitrary", "arbitrary"), vmem_limit_bytes=VMEM_LIMIT),
        name="mixer_prompt",
    )(x2d, *weights)


def _mixer_sample(x2d, row_block, weights, p1, p2):
    full = lambda shape: pl.BlockSpec(shape, lambda i: (0,) * len(shape))
    return pl.pallas_call(
        _mixer_sample_kernel,
        grid=(1,),
        in_specs=[pl.BlockSpec((T_SAMPLE, D_MODEL), lambda i: (row_block, 0))]
        + _mixer_weight_specs(1)
        + [full((T_SAMPLE, D_B)), full((T_SAMPLE, D_B))],
        out_specs=[full((T_SAMPLE, D_MODEL)), full((T_SAMPLE, D_A)), full((T_SAMPLE, D_B))],
        out_shape=[
            jax.ShapeDtypeStruct((T_SAMPLE, D_MODEL), F32),
            jax.ShapeDtypeStruct((T_SAMPLE, D_A), F32),
            jax.ShapeDtypeStruct((T_SAMPLE, D_B), F32),
        ],
        scratch_shapes=[
            pltpu.VMEM((T_SAMPLE, D_A), BF16),
            pltpu.VMEM((T_SAMPLE + SUBLANES, D_B), F32),
        ],
        compiler_params=pltpu.CompilerParams(
            dimension_semantics=("arbitrary",), vmem_limit_bytes=VMEM_LIMIT),
        name="mixer_sample",
    )(x2d, *weights, p1, p2)


def _cmpx(v, i, j):
    hi = jnp.maximum(v[i], v[j])
    lo = jnp.minimum(v[i], v[j])
    v[i] = hi
    v[j] = lo


def _bitonic_merge16(v):
    j = TOPK // 2
    while j >= 1:
        for i in range(TOPK):
            l = i ^ j
            if l > i:
                _cmpx(v, i, l)
        j //= 2


def _sort16(v):
    k = 2
    while k <= TOPK:
        j = k // 2
        while j >= 1:
            for i in range(TOPK):
                l = i ^ j
                if l > i:
                    if (i & k) == 0:
                        _cmpx(v, i, l)
                    else:
                        _cmpx(v, l, i)
            j //= 2
        k *= 2


def _top16(s):
    v = [s[k] for k in range(TOPK)]
    _sort16(v)
    for shift in (4, 2, 1):
        w = [pltpu.roll(v[k], shift, 0) for k in range(TOPK)]
        v = [jnp.maximum(v[k], w[TOPK - 1 - k]) for k in range(TOPK)]
        _bitonic_merge16(v)
    return v


def _sublane_allreduce(x, op):
    for shift in (4, 2, 1):
        x = op(x, pltpu.roll(x, shift, 0))
    return x


def _on_sublanes(vals, sub):
    out = vals[SUBLANES - 1]
    for r in range(SUBLANES - 2, -1, -1):
        out = jnp.where(sub == r, vals[r], out)
    return out


def _bf16_row_pairs(x):
    return pltpu.bitcast(x.reshape(N_KEYS, LANES).astype(BF16), jnp.uint32)


def _count_above(t, x, above):
    bits = []
    for step in range(4):
        width = TOPK >> (step + 1)
        pivots = [t[base + width - 1] for base in range(0, TOPK, 2 * width)]
        for bit in reversed(bits):
            pivots = [jnp.where(bit, hi, lo) for lo, hi in zip(pivots[0::2], pivots[1::2])]
        pivot = pivots[0]
        bits.append(above(pivot[None] if pivot.ndim == 2 else pivot, x))
    count = jnp.where(above(t[TOPK - 1][None], x), 1.0, 0.0)
    for step, bit in enumerate(bits):
        count = count + jnp.where(bit, float(TOPK >> (step + 1)), 0.0)
    return count


def _select_kernel(x1p_ref, x1s_ref, ln2_ref, wqt_ref, k1_ref, k2_ref,
                   h2t_ref, r2_ref, e2_ref, c1_ref, p1_ref, s1_scr, s2_scr):
    tokens = SEL_TOKENS
    is_sample = pl.program_id(0) == T_PROMPT // SEL_TOKENS
    h2 = _rms(jnp.where(is_sample, x1s_ref[...], x1p_ref[...]), ln2_ref[...])
    h2t = h2.T.astype(BF16)
    h2t_ref[...] = pltpu.bitcast(h2t, jnp.uint32)
    qt = _dot(wqt_ref[...], h2t)
    for h in range(HEADS):
        q1 = qt[h * D_QUERY:h * D_QUERY + D_HALF].astype(BF16)
        q2 = qt[h * D_QUERY + D_HALF:(h + 1) * D_QUERY].astype(BF16)
        s1_scr[h] = _dot(k1_ref[...], q1).reshape(TOPK, SUBLANES, tokens)
        s2_scr[h] = _dot(k2_ref[...], q2).reshape(TOPK, SUBLANES, tokens)

    sub = lax.broadcasted_iota(jnp.int32, (SUBLANES, LANES), 0)

    def lane_block(lb, carry):
        lanes = pl.ds(pl.multiple_of(lb * LANES, LANES), LANES)
        for h in range(HEADS):
            s1 = s1_scr[h, :, :, lanes]
            s2 = s2_scr[h, :, :, lanes]
            t1 = _top16(s1)
            t2 = _top16(s2)
            t2lo = _on_sublanes(t2[0:8], sub)
            t2hi = _on_sublanes(t2[8:16], sub)
            t1hi = _on_sublanes(t1[8:16], sub)
            cands = [t1[0] + t2lo, t1[0] + t2hi, t1[1] + t2lo]
            for a, nb in ((2, 5), (3, 4), (4, 3), (5, 2), (6, 2), (7, 2)):
                cands.append(jnp.where(sub < nb, t1[a] + t2lo, NEG))
            cands.append(t1hi + t2[0])
            cur = list(cands)
            tops = []
            for it in range(TOPK + 1):
                m = functools.reduce(jnp.maximum, cur)
                m = _sublane_allreduce(m, jnp.maximum)
                tops.append(m)
                if it < TOPK:
                    cur = [jnp.where(c == m, NEG, c) for c in cur]
            tau = 0.5 * (tops[TOPK - 1] + tops[TOPK])
            z = functools.reduce(
                jnp.add, [jnp.where(c >= tau, jnp.exp(c - tops[0]), 0.0) for c in cands])
            z = _sublane_allreduce(z, jnp.add)
            inv_z = 1.0 / z
            p1 = jnp.where(s1 >= t1[TOPK - 1][None], jnp.exp(s1 - t1[0][None]) * inv_z[None], 0.0)
            e2 = jnp.where(s2 >= t2[TOPK - 1][None], jnp.exp(s2 - t2[0][None]), 0.0)
            th = tau[None] - s1
            rank2 = _count_above(t2, s2, jnp.greater)
            r2_ref[h, lb] = _bf16_row_pairs(rank2)
            e2_ref[h, lb] = _bf16_row_pairs(e2)
            c1_ref[h, lb] = _count_above(t2, th, jnp.greater_equal)
            p1_ref[h, lb] = p1
        return carry

    lax.fori_loop(0, tokens // LANES, lane_block, 0)


def _select(x1_p, x1_s, ln2, wqt, k1, k2):
    nb = T_ALL // SEL_TOKENS
    n_prompt = T_PROMPT // SEL_TOKENS
    assert T_SAMPLE == SEL_TOKENS and nb == n_prompt + 1
    lb_per_step = SEL_TOKENS // LANES
    n_lb = T_ALL // LANES
    key_shape = jax.ShapeDtypeStruct((HEADS, n_lb, TOPK, SUBLANES, LANES), F32)
    key_spec = pl.BlockSpec((HEADS, lb_per_step, TOPK, SUBLANES, LANES), lambda i: (0, i, 0, 0, 0))
    pair_shape = jax.ShapeDtypeStruct((HEADS, n_lb, N_KEYS // 2, LANES), jnp.uint32)
    pair_spec = pl.BlockSpec((HEADS, lb_per_step, N_KEYS // 2, LANES), lambda i: (0, i, 0, 0))
    scr = pltpu.VMEM((HEADS, TOPK, SUBLANES, SEL_TOKENS), F32)
    return pl.pallas_call(
        _select_kernel,
        grid=(nb,),
        in_specs=[
            pl.BlockSpec((SEL_TOKENS, D_MODEL), lambda i: (jnp.minimum(i, n_prompt - 1), 0)),
            pl.BlockSpec((SEL_TOKENS, D_MODEL), lambda i: (0, 0)),
            _const_spec((1, D_MODEL), 1),
            _const_spec((HEADS * D_QUERY, D_MODEL), 1),
            _const_spec((N_KEYS, D_HALF), 1),
            _const_spec((N_KEYS, D_HALF), 1),
        ],
        out_specs=[pl.BlockSpec((D_MODEL // 2, SEL_TOKENS), lambda i: (0, i)),
                   pair_spec, pair_spec, key_spec, key_spec],
        out_shape=[jax.ShapeDtypeStruct((D_MODEL // 2, T_ALL), jnp.uint32),
                   pair_shape, pair_shape, key_shape, key_shape],
        scratch_shapes=[scr, scr],
        compiler_params=pltpu.CompilerParams(
            dimension_semantics=("arbitrary",), vmem_limit_bytes=VMEM_LIMIT),
        name="peer_select",
    )(x1_p, x1_s, ln2, wqt, k1, k2)


def _peer_kernel(final, h2t_ref, r2a_ref, e2a_ref, r2b_ref, e2b_ref, c1a_ref, p1a_ref, c1b_ref, p1b_ref,
                 u_ref, vt_ref,
                 x1p_ref, x1s_ref, pep_ref, pes_ref, ln3_ref, wpg_ref, wple_ref, lnf_ref, *out_and_scratch):
    n_out = 2 if final else 1
    o_refs = out_and_scratch[:n_out]
    acc_ref, at0_ref, at1_ref, ga0_ref, ga1_ref = out_and_scratch[n_out:]
    _peer_body(final, h2t_ref, r2a_ref, e2a_ref, r2b_ref, e2b_ref, c1a_ref, p1a_ref, c1b_ref, p1b_ref,
               u_ref, vt_ref,
               x1p_ref, x1s_ref, pep_ref, pes_ref, ln3_ref, wpg_ref, wple_ref, lnf_ref, o_refs,
               acc_ref, at0_ref, at1_ref, ga0_ref, ga1_ref)


def _peer_body(final, h2t_ref, r2a_ref, e2a_ref, r2b_ref, e2b_ref, c1a_ref, p1a_ref, c1b_ref, p1b_ref,
               u_ref, vt_ref,
               x1p_ref, x1s_ref, pep_ref, pes_ref, ln3_ref, wpg_ref, wple_ref, lnf_ref, o_refs,
               acc_ref, at0_ref, at1_ref, ga0_ref, ga1_ref):
    j = pl.program_id(0)
    phase = j % PEER_STEPS_PER_BLOCK
    rows_half = HALF_EXPERTS // N_KEYS
    vregs = GATE_KEYS // BF16_ROWS

    @pl.when(j == 0)
    def _():
        at1_ref[...] = jnp.zeros_like(at1_ref)
        ga0_ref[...] = jnp.zeros_like(ga0_ref)
        ga1_ref[...] = jnp.zeros_like(ga1_ref)

    @pl.when((phase == 1) | (j == 0))
    def _():
        acc_ref[...] = jnp.zeros_like(acc_ref)

    def lane_bcast(ref, h, lb, r):
        return jnp.broadcast_to(ref[h, lb, r:r + 1, :], (BF16_ROWS, LANES)).astype(BF16)

    def gate_tile(at_ref, ga_ref, r2_ref, e2_ref, c1_ref, p1_ref, row0, lb, rg, kq):
        lanes = slice(lb * LANES, (lb + 1) * LANES)
        keys = [slice(kq * GATE_KEYS + k * BF16_ROWS, kq * GATE_KEYS + (k + 1) * BF16_ROWS)
                for k in range(vregs)]
        key_rows = range(rg * GATE_ROWS, (rg + 1) * GATE_ROWS)
        g = {r: [None] * vregs for r in key_rows}
        for h in range(HEADS):
            r2 = [pltpu.bitcast(r2_ref[h, lb, ks.start // 2:ks.stop // 2, :], BF16) for ks in keys]
            e2 = [pltpu.bitcast(e2_ref[h, lb, ks.start // 2:ks.stop // 2, :], BF16) for ks in keys]
            for r in key_rows:
                c = lane_bcast(c1_ref, h, lb, row0 + r)
                p = lane_bcast(p1_ref, h, lb, row0 + r)
                for k in range(vregs):
                    term = jnp.where(r2[k] < c, e2[k], jnp.zeros_like(e2[k])) * p
                    g[r][k] = term if h == 0 else g[r][k] + term
        for r in key_rows:
            for k in range(vregs):
                rows = slice(r * N_KEYS + keys[k].start, r * N_KEYS + keys[k].stop)
                a = at_ref[rows, lanes].astype(BF16)
                ga_ref[rows, lanes] = g[r][k] * jax.nn.gelu(a)

    def half_step(half, at_new, at_cur, ga_new, ga_old, r2_ref, e2_ref, c1_ref, p1_ref, row0):
        u_rows = slice(half * HALF_EXPERTS // 2, (half + 1) * HALF_EXPERTS // 2)
        v_cols = slice(half * HALF_EXPERTS, (half + 1) * HALF_EXPERTS)
        def project(cols):
            at_new[:, cols] = _dot(pltpu.bitcast(u_ref[u_rows, :], BF16),
                                   pltpu.bitcast(h2t_ref[:, cols], BF16))

        def accumulate(cols):
            acc_ref[:, cols] += _dot(pltpu.bitcast(vt_ref[:, v_cols], BF16), ga_old[:, cols])

        halves = [slice(n * PEER_TOKENS // 2, (n + 1) * PEER_TOKENS // 2) for n in range(2)]
        pieces = [(stage, cols) for stage in (project, accumulate) for cols in halves]
        tiles = [(lb, rg, kq) for lb in range(PEER_TOKENS // LANES)
                 for rg in range(rows_half // GATE_ROWS) for kq in range(N_KEYS // GATE_KEYS)]
        per_piece = len(tiles) // len(pieces)
        assert per_piece * len(pieces) == len(tiles)

        @pl.when(j > -1 - half)
        def _():
            for n, (stage, cols) in enumerate(pieces):
                mine = tiles[n * per_piece:(n + 1) * per_piece]
                for lb, rg, kq in mine[:3 * per_piece // 4]:
                    gate_tile(at_cur, ga_new, r2_ref, e2_ref, c1_ref, p1_ref, row0, lb, rg, kq)
                stage(cols)
                for lb, rg, kq in mine[3 * per_piece // 4:]:
                    gate_tile(at_cur, ga_new, r2_ref, e2_ref, c1_ref, p1_ref, row0, lb, rg, kq)

    half_step(0, at0_ref, at1_ref, ga1_ref, ga0_ref, r2a_ref, e2a_ref, c1a_ref, p1a_ref,
              rows_half % SUBLANES)

    half_step(1, at1_ref, at0_ref, ga0_ref, ga1_ref, r2b_ref, e2b_ref, c1b_ref, p1b_ref, 0)

    @pl.when((phase == 0) & (j > 0))
    def _():
        is_sample = (j - 1) // PEER_STEPS_PER_BLOCK == T_PROMPT // PEER_TOKENS
        pe = jnp.where(is_sample, pes_ref[...], pep_ref[...])
        x2 = jnp.where(is_sample, x1s_ref[...], x1p_ref[...]) + acc_ref[...].T
        h3 = _rms(x2, ln3_ref[...]).astype(BF16)
        gate = _sigmoid(_dot(h3, wpg_ref[...]))
        x3 = x2 + gate * _dot(pe.astype(BF16), wple_ref[...])
        if final:
            y = _rms(x3, lnf_ref[...])

            @pl.when(jnp.logical_not(is_sample))
            def _():
                o_refs[0][...] = y

            @pl.when(is_sample)
            def _():
                o_refs[1][...] = y
        else:
            o_refs[0][...] = x3


def _peer(final, layer, h2t, r2, e2, c1, p1, u_pk, vt_pk, x1_p, x1_s, pe_p, pe_s, ln3, wpg, wple, lnf):
    nt = T_ALL // PEER_TOKENS
    n_prompt = T_PROMPT // PEER_TOKENS
    assert T_SAMPLE == PEER_TOKENS and nt == n_prompt + 1
    tiles_per_block = N_EXPERTS // HALF_EXPERTS
    n_tiles = nt * tiles_per_block
    assert tiles_per_block == 2 * PEER_STEPS_PER_BLOCK
    rows_half = HALF_EXPERTS // N_KEYS
    assert SUBLANES % rows_half == 0
    n_steps = n_tiles // 2 + 1
    lb_per_step = PEER_TOKENS // LANES

    prev_step = lambda j: jnp.maximum(j - 1, 0)
    tok_spec = lambda width: pl.BlockSpec(
        (PEER_TOKENS, width), lambda j: (prev_step(j) // PEER_STEPS_PER_BLOCK, 0))
    cur_block = lambda j: jnp.minimum(j // PEER_STEPS_PER_BLOCK, nt - 1)

    def gate_specs(tile_of_step):
        tile = lambda j: jnp.clip(tile_of_step(j), 0, n_tiles - 1)
        pairs = pl.BlockSpec((HEADS, lb_per_step, N_KEYS // 2, LANES),
                             lambda j: (0, tile(j) // tiles_per_block, 0, 0))
        rows = pl.BlockSpec(
            (HEADS, lb_per_step, None, SUBLANES, LANES),
            lambda j: (0, tile(j) // tiles_per_block,
                       (tile(j) % tiles_per_block) * rows_half // SUBLANES, 0, 0))
        return pairs, rows

    pairs_a, rows_a = gate_specs(lambda j: 2 * j - 1)
    pairs_b, rows_b = gate_specs(lambda j: 2 * j)
    const = lambda shape: pl.BlockSpec(shape, lambda j: (0,) * len(shape), pipeline_mode=pl.Buffered(1))
    prompt_block = lambda j: jnp.minimum(prev_step(j) // PEER_STEPS_PER_BLOCK, n_prompt - 1)
    if final:
        out_specs = [pl.BlockSpec((PEER_TOKENS, D_MODEL), lambda j: (prompt_block(j), 0)),
                     pl.BlockSpec((PEER_TOKENS, D_MODEL), lambda j: (0, 0))]
        out_shape = [jax.ShapeDtypeStruct((T_PROMPT, D_MODEL), F32),
                     jax.ShapeDtypeStruct((T_SAMPLE, D_MODEL), F32)]
    else:
        out_specs = [tok_spec(D_MODEL)]
        out_shape = [jax.ShapeDtypeStruct((T_ALL, D_MODEL), F32)]
    return pl.pallas_call(
        functools.partial(_peer_kernel, final),
        grid=(n_steps,),
        in_specs=[
            pl.BlockSpec((D_MODEL // 2, PEER_TOKENS), lambda j: (0, cur_block(j))),
            pairs_a, pairs_a, pairs_b, pairs_b, rows_a, rows_a, rows_b, rows_b,
            pl.BlockSpec((None, HALF_EXPERTS, D_MODEL), lambda j: (layer, j % PEER_STEPS_PER_BLOCK, 0)),
            pl.BlockSpec((None, D_MODEL // 2, 2 * HALF_EXPERTS),
                         lambda j: (layer, 0, prev_step(j) % PEER_STEPS_PER_BLOCK)),
            pl.BlockSpec((PEER_TOKENS, D_MODEL), lambda j: (prompt_block(j), 0)),
            pl.BlockSpec((PEER_TOKENS, D_MODEL), lambda j: (0, 0), pipeline_mode=pl.Buffered(1)),
            pl.BlockSpec((None, PEER_TOKENS, D_PLE), lambda j: (layer, prompt_block(j), 0)),
            pl.BlockSpec((None, PEER_TOKENS, D_PLE), lambda j: (layer, 0, 0),
                         pipeline_mode=pl.Buffered(1)),
            const((1, D_MODEL)), const((D_MODEL, D_MODEL)), const((D_PLE, D_MODEL)), const((1, D_MODEL)),
        ],
        out_specs=out_specs,
        out_shape=out_shape,
        scratch_shapes=[pltpu.VMEM((D_MODEL, PEER_TOKENS), F32)]
        + [pltpu.VMEM((HALF_EXPERTS, PEER_TOKENS), F32)] * 2
        + [pltpu.VMEM((HALF_EXPERTS, PEER_TOKENS), BF16)] * 2,
        compiler_params=pltpu.CompilerParams(
            dimension_semantics=("arbitrary",), vmem_limit_bytes=VMEM_LIMIT),
        name="peer_dense",
    )(h2t, r2, e2, r2, e2, c1, p1, c1, p1, u_pk, vt_pk, x1_p, x1_s, pe_p, pe_s, ln3, wpg, wple, lnf)


def _pack_tables_kernel(u_ref, v_ref, upk_ref, vtpk_ref):
    upk_ref[...] = pltpu.bitcast(u_ref[...].astype(BF16), jnp.uint32)
    vtpk_ref[...] = pltpu.bitcast(v_ref[...].T.astype(BF16), jnp.uint32)


def _pack_tables(u_tab, v_tab):
    nb = N_EXPERTS // PACK_EXPERTS
    in_spec = pl.BlockSpec((None, PACK_EXPERTS, D_MODEL), lambda l, e: (l, e, 0))
    return pl.pallas_call(
        _pack_tables_kernel,
        grid=(DEPTH, nb),
        in_specs=[in_spec, in_spec],
        out_specs=[
            pl.BlockSpec((None, PACK_EXPERTS // 2, D_MODEL), lambda l, e: (l, e, 0)),
            pl.BlockSpec((None, D_MODEL // 2, PACK_EXPERTS), lambda l, e: (l, 0, e)),
        ],
        out_shape=[
            jax.ShapeDtypeStruct((DEPTH, N_EXPERTS // 2, D_MODEL), jnp.uint32),
            jax.ShapeDtypeStruct((DEPTH, D_MODEL // 2, N_EXPERTS), jnp.uint32),
        ],
        compiler_params=pltpu.CompilerParams(
            dimension_semantics=("arbitrary", "arbitrary"), vmem_limit_bytes=VMEM_LIMIT),
        name="pack_tables",
    )(u_tab, v_tab)


def kernel(x_prompt, x_sample, state_conv, p_prompt, p_sample, ln1, w_in, ln_v, w_s, b_s, conv_w,
           w_a, w_b, w_o, ln2, w_q, k1, k2, u_tab, v_tab, ln3, w_pg, w_ple, ln_f):
    assert x_prompt.shape == (N_PROMPT_SEQ, PROMPT_LEN, D_MODEL)
    assert x_sample.shape == (N_SAMPLE_SEQ, SAMPLE_LEN, D_MODEL)
    assert u_tab.shape == (DEPTH, N_EXPERTS, D_MODEL)

    x_p = x_prompt.reshape(T_PROMPT, D_MODEL)
    x_s = x_sample.reshape(T_SAMPLE, D_MODEL)
    sample_block = 0
    seqs_per_chunk = CHUNK // SAMPLE_LEN
    eye = jnp.eye(seqs_per_chunk, dtype=F32)
    lnf = ln_f.reshape(1, D_MODEL)
    u_pk, vt_pk = _pack_tables(u_tab, v_tab)
    pe_p = p_prompt.reshape(DEPTH, T_PROMPT, D_PLE)
    pe_s = p_sample.reshape(DEPTH, T_SAMPLE, D_PLE)
    conv_prompt, conv_sample, v_sample = [], [], []
    y = None

    for i in range(DEPTH):
        row = lambda a: a[i].reshape(1, -1)
        ws_p = w_s[i].astype(BF16)
        corner = w_s[i][:, :SAMPLE_LEN, :SAMPLE_LEN]
        ws_s = (eye[None, :, None, :, None] * corner[:, None, :, None, :]).reshape(
            N_GROUPS, CHUNK, CHUNK).astype(BF16)
        bias_p = jnp.repeat(b_s[i].T, GROUP, axis=1)
        bias_s = jnp.tile(jnp.repeat(b_s[i][:, :SAMPLE_LEN].T, GROUP, axis=1), (seqs_per_chunk, 1))
        shared = (conv_w[i], w_a[i].astype(BF16), w_b[i].astype(BF16), w_o[i].astype(BF16))
        front = (row(ln1), w_in[i].astype(BF16), row(ln_v))
        w_prompt = front + (ws_p, bias_p) + shared
        w_sample = front + (ws_s, bias_s) + shared

        prev = state_conv[i]
        zero = jnp.zeros((N_SAMPLE_SEQ, 1, D_B), F32)
        tap1 = jnp.concatenate([prev[:, 1:2], zero, zero, zero], axis=1).reshape(T_SAMPLE, D_B)
        tap2 = jnp.concatenate([prev[:, 0:1], prev[:, 1:2], zero, zero], axis=1).reshape(T_SAMPLE, D_B)

        x1_p, tail_p = _mixer_prompt(x_p, w_prompt)
        x1_s, v_s, cx_s = _mixer_sample(x_s, sample_block, w_sample, tap1, tap2)
        conv_prompt.append(tail_p)
        conv_sample.append(cx_s.reshape(N_SAMPLE_SEQ, SAMPLE_LEN, D_B)[:, SAMPLE_LEN - 2:])
        v_sample.append(v_s.reshape(N_SAMPLE_SEQ, SAMPLE_LEN, D_A))

        h2t, r2, e2, c1, p1 = _select(x1_p, x1_s, row(ln2), w_q[i].T.astype(BF16),
                                      k1[i].astype(BF16), k2[i].astype(BF16))
        final = i == DEPTH - 1
        y = _peer(final, i, h2t, r2, e2, c1, p1, u_pk, vt_pk,
                  x1_p, x1_s, pe_p, pe_s, row(ln3), w_pg[i].astype(BF16), w_ple[i].astype(BF16), lnf)
        x_p = y[0]
        x_s = y[0]
        sample_block = T_PROMPT // T_SAMPLE

    y_prompt = y[0].reshape(N_PROMPT_SEQ, PROMPT_LEN, D_MODEL)
    y_sample = y[1].reshape(N_SAMPLE_SEQ, SAMPLE_LEN, D_MODEL)
    return (y_prompt, y_sample, jnp.stack(conv_prompt), jnp.stack(conv_sample), jnp.stack(v_sample))
```
